```python
import jax, jax.numpy as jnp
from jax import lax
import numpy as np

D_MODEL = 4096
BATCH = 4
SEQ = 2048
DEPTH = 1

PLE_DIM = 256
CONV_WIDTH = 4
RG_WIDTH = D_MODEL // 2
RG_BLOCK = 128
RG_BLOCKS = RG_WIDTH // RG_BLOCK
RG_C = 8.0
GDN_HEAD_DIM = 128
GDN_HEADS = (D_MODEL // 2) // GDN_HEAD_DIM
GDN_WIDTH = GDN_HEADS * GDN_HEAD_DIM
GDN_CHUNK = 64
PEER_HEADS = 8
PEER_NKEYS = 128
PEER_EXPERTS = PEER_NKEYS * PEER_NKEYS
PEER_DKEY = 256
PEER_HALF = PEER_DKEY // 2
PEER_TOPK = 16
PEER_TOKEN_BLOCK = 64
DN_ALPHA = (2.0 * DEPTH) ** 0.25
DN_BETA = (8.0 * DEPTH) ** -0.25
LN_EPS = 1e-5
NORM_EPS = 1e-6
IN_WIDTHS = (RG_WIDTH, RG_WIDTH, GDN_WIDTH, GDN_WIDTH, GDN_WIDTH, GDN_WIDTH, GDN_HEADS, GDN_HEADS, D_MODEL, D_MODEL)
IN_COLS = sum(IN_WIDTHS)

kernel_name = "hybrid_rglru_gdn_peer_deepnorm"


def split_cols(t, widths):
    out = []
    start = 0
    for w in widths:
        out.append(t[..., start:start + w])
        start += w
    return out


def layer_norm(x, g, b):
    xf = x.astype(jnp.float32)
    mu = jnp.mean(xf, axis=-1, keepdims=True)
    var = jnp.mean(jnp.square(xf - mu), axis=-1, keepdims=True)
    return ((xf - mu) * lax.rsqrt(var + LN_EPS) * g + b).astype(x.dtype)


def l2norm(t):
    tf = t.astype(jnp.float32)
    return tf * lax.rsqrt(jnp.sum(tf * tf, axis=-1, keepdims=True) + NORM_EPS)


def causal_dwconv(x, w, b=None):
    K = w.shape[0]
    T = x.shape[1]
    xp = jnp.pad(x, ((0, 0), (K - 1, 0), (0, 0)))
    y = xp[:, 0:T] * w[0]
    for j in range(1, K):
        y = y + xp[:, j:j + T] * w[j]
    if b is not None:
        y = y + b
    return y


def rg_lru(xc, wa, ba, wx, bx, lam):
    bsz, seq, _ = xc.shape
    xf = xc.astype(jnp.float32)
    xb = xf.reshape(bsz, seq, RG_BLOCKS, RG_BLOCK)
    r = jax.nn.sigmoid(jnp.einsum('bthi,hij->bthj', xb, wa.astype(jnp.float32)).reshape(bsz, seq, RG_WIDTH) + ba)
    i = jax.nn.sigmoid(jnp.einsum('bthi,hij->bthj', xb, wx.astype(jnp.float32)).reshape(bsz, seq, RG_WIDTH) + bx)
    log_a = -RG_C * r * jax.nn.softplus(-lam.astype(jnp.float32))
    a = jnp.exp(log_a)
    b = jnp.sqrt(-jnp.expm1(2.0 * log_a)) * (i * xf)

    def combine(left, right):
        al, bl = left
        ar, br = right
        return al * ar, ar * bl + br

    _, h = lax.associative_scan(combine, (a, b), axis=1)
    return h


def chunk_gated_delta_rule(q, k, v, g, beta):
    bsz, seq, nh, dk = q.shape
    dv = v.shape[-1]
    nc = seq // GDN_CHUNK
    f32 = jnp.float32

    def to_chunks(t):
        t = t.astype(f32).reshape((bsz, nc, GDN_CHUNK, nh) + t.shape[3:])
        return jnp.moveaxis(t, 3, 1)

    q = to_chunks(q) * (dk ** -0.5)
    k = to_chunks(k)
    v = to_chunks(v)
    g = to_chunks(g)
    beta = to_chunks(beta)

    gc = jnp.cumsum(g, axis=-1)
    idx = jnp.arange(GDN_CHUNK)
    causal = idx[:, None] >= idx[None, :]
    strict = idx[:, None] > idx[None, :]
    decay = jnp.exp(jnp.where(causal, gc[..., :, None] - gc[..., None, :], -jnp.inf))

    kb = k * beta[..., None]
    a_low = jnp.where(strict, jnp.einsum('bhncd,bhnsd->bhncs', kb, k) * decay, 0.0)
    lhs = a_low + jnp.eye(GDN_CHUNK, dtype=f32)
    rhs = jnp.concatenate([v * beta[..., None], kb * jnp.exp(gc)[..., None]], axis=-1)
    sol = lax.linalg.triangular_solve(lhs, rhs, left_side=True, lower=True, unit_diagonal=True)
    u_c = sol[..., :dv]
    w_c = sol[..., dv:]

    attn = jnp.einsum('bhncd,bhnsd->bhncs', q, k) * decay
    q_g = q * jnp.exp(gc)[..., None]
    k_d = k * jnp.exp(gc[..., -1:] - gc)[..., None]
    last = jnp.exp(gc[..., -1])

    def step(state, xs):
        q_n, k_n, u_n, w_n, a_n, l_n = xs
        v_new = u_n - jnp.einsum('bhck,bhkv->bhcv', w_n, state)
        o_n = jnp.einsum('bhck,bhkv->bhcv', q_n, state) + jnp.einsum('bhcs,bhsv->bhcv', a_n, v_new)
        state = state * l_n[..., None, None] + jnp.einsum('bhck,bhcv->bhkv', k_n, v_new)
        return state, o_n

    xs = tuple(jnp.moveaxis(t, 2, 0) for t in (q_g, k_d, u_c, w_c, attn, last))
    s0 = jnp.zeros((bsz, nh, dk, dv), f32)
    _, o = lax.scan(step, s0, xs)
    return jnp.transpose(o, (1, 0, 3, 2, 4)).reshape(bsz, seq, nh, dv)


def gated_rmsnorm(o, z, w):
    of = o.astype(jnp.float32)
    of = of * lax.rsqrt(jnp.mean(of * of, axis=-1, keepdims=True) + NORM_EPS)
    return of * w * jax.nn.silu(z.astype(jnp.float32))


def token_mixer(u, w_in, rg_conv_w, rg_conv_b, rg_wa, rg_ba, rg_wx, rg_bx, rg_lambda, rg_out,
                gdn_conv_w, gdn_a_log, gdn_dt_bias, gdn_norm_w, gdn_out, w_o):
    bsz, seq, _ = u.shape
    dt = u.dtype
    proj = jnp.einsum('btd,dc->btc', u, w_in)
    rg_x, rg_y, q, k, v, z, b_gate, a_gate, m_a, m_b = split_cols(proj, IN_WIDTHS)

    xr = causal_dwconv(rg_x, rg_conv_w, rg_conv_b)
    h = rg_lru(xr, rg_wa, rg_ba, rg_wx, rg_bx, rg_lambda)
    h = h * jax.nn.gelu(rg_y.astype(jnp.float32), approximate=False)
    y_a = jnp.einsum('btc,cd->btd', h.astype(dt), rg_out)

    qkv = jax.nn.silu(causal_dwconv(jnp.concatenate([q, k, v], axis=-1), gdn_conv_w))
    q, k, v = split_cols(qkv, (GDN_WIDTH, GDN_WIDTH, GDN_WIDTH))
    heads = (bsz, seq, GDN_HEADS, GDN_HEAD_DIM)
    q = l2norm(q.reshape(heads))
    k = l2norm(k.reshape(heads))
    v = v.reshape(heads)
    beta = jax.nn.sigmoid(b_gate.astype(jnp.float32))
    g = -jnp.exp(gdn_a_log.astype(jnp.float32)) * jax.nn.softplus(a_gate.astype(jnp.float32) + gdn_dt_bias.astype(jnp.float32))
    o = chunk_gated_delta_rule(q, k, v, g, beta)
    o = gated_rmsnorm(o, z.reshape(heads), gdn_norm_w.astype(jnp.float32)).reshape(bsz, seq, GDN_WIDTH)
    y_b = jnp.einsum('btc,cd->btd', o.astype(dt), gdn_out)

    merged = jax.nn.sigmoid(m_a) * y_a + jax.nn.sigmoid(m_b) * y_b
    return jnp.einsum('btd,de->bte', merged, w_o)


def peer(x, w_q, sub_keys, u_tab, v_tab):
    bsz, seq, dm = x.shape
    m = bsz * seq
    tok = x.reshape(m, dm)
    q = jnp.einsum('md,dc->mc', tok, w_q).reshape(m, PEER_HEADS, 2, PEER_HALF)
    s = jnp.einsum('mhpd,hpkd->mhpk', q.astype(jnp.float32), sub_keys.astype(jnp.float32))
    s_top, i_top = lax.top_k(s, PEER_TOPK)
    cand = (s_top[:, :, 0, :, None] + s_top[:, :, 1, None, :]).reshape(m, PEER_HEADS, PEER_TOPK * PEER_TOPK)
    cand_id = (i_top[:, :, 0, :, None] * PEER_NKEYS + i_top[:, :, 1, None, :]).reshape(m, PEER_HEADS, PEER_TOPK * PEER_TOPK)
    best_s, best_j = lax.top_k(cand, PEER_TOPK)
    ids = jnp.take_along_axis(cand_id, best_j, axis=-1)
    gates = jax.nn.softmax(best_s, axis=-1)

    nb = m // PEER_TOKEN_BLOCK
    n_sel = PEER_HEADS * PEER_TOPK
    xs = tok.reshape(nb, PEER_TOKEN_BLOCK, dm)
    ib = ids.reshape(nb, PEER_TOKEN_BLOCK, n_sel)
    gb = gates.reshape(nb, PEER_TOKEN_BLOCK, n_sel)

    def expert_block(args):
        xb, idb, gtb = args
        hid = jnp.einsum('ted,td->te', u_tab[idb], xb)
        wgt = gtb * jax.nn.gelu(hid.astype(jnp.float32), approximate=False)
        return jnp.einsum('te,ted->td', wgt.astype(xb.dtype), v_tab[idb])

    out = lax.map(expert_block, (xs, ib, gb))
    return out.reshape(bsz, seq, dm)


def setup_inputs(seed: int = 0) -> dict:
    key = jax.random.key(seed)
    ks = iter(jax.random.split(key, 40))
    f32 = jnp.float32
    L = DEPTH
    D = D_MODEL

    def nrm(shape, scale):
        return jax.random.normal(next(ks), shape, f32) * scale

    def gain(shape):
        return 1.0 + nrm(shape, 0.02)

    x = nrm((BATCH, SEQ, D), 1.0)
    p = nrm((L, BATCH, SEQ, PLE_DIM), 1.0)
    ln_emb_g = gain((D,))
    ln_emb_b = nrm((D,), 0.02)
    w_in = nrm((L, D, IN_COLS), D ** -0.5)
    rg_conv_w = nrm((L, CONV_WIDTH, RG_WIDTH), CONV_WIDTH ** -0.5)
    rg_conv_b = nrm((L, RG_WIDTH), 0.02)
    rg_wa = nrm((L, RG_BLOCKS, RG_BLOCK, RG_BLOCK), RG_BLOCK ** -0.5)
    rg_ba = nrm((L, RG_WIDTH), 0.02)
    rg_wx = nrm((L, RG_BLOCKS, RG_BLOCK, RG_BLOCK), RG_BLOCK ** -0.5)
    rg_bx = nrm((L, RG_WIDTH), 0.02)
    a_pow = jax.random.uniform(next(ks), (L, RG_WIDTH), f32, 0.9, 0.999)
    s = a_pow ** (1.0 / RG_C)
    rg_lambda = jnp.log(s) - jnp.log1p(-s)
    rg_out = nrm((L, RG_WIDTH, D), DN_BETA * RG_WIDTH ** -0.5)
    gdn_conv_w = nrm((L, CONV_WIDTH, 3 * GDN_WIDTH), CONV_WIDTH ** -0.5)
    gdn_a_log = jnp.log(jax.random.uniform(next(ks), (L, GDN_HEADS), f32, 1.0, 16.0))
    dt0 = jnp.exp(jax.random.uniform(next(ks), (L, GDN_HEADS), f32, float(np.log(1e-3)), float(np.log(1e-1))))
    gdn_dt_bias = dt0 + jnp.log(-jnp.expm1(-dt0))
    gdn_norm_w = gain((L, GDN_HEAD_DIM))
    gdn_out = nrm((L, GDN_WIDTH, D), DN_BETA * GDN_WIDTH ** -0.5)
    w_o = nrm((L, D, D), DN_BETA * D ** -0.5)
    ln1_g = gain((L, D))
    ln1_b = nrm((L, D), 0.02)
    peer_wq = nrm((L, D, PEER_HEADS * PEER_DKEY), D ** -0.5)
    peer_keys = nrm((L, PEER_HEADS, 2, PEER_NKEYS, PEER_HALF), PEER_HALF ** -0.5)
    peer_u = nrm((L, PEER_EXPERTS, D), D ** -0.5)
    peer_v = nrm((L, PEER_EXPERTS, D), DN_BETA)
    ple_w = nrm((L, PLE_DIM, D), DN_BETA * PLE_DIM ** -0.5)
    ple_gate_w = nrm((L, D, D), D ** -0.5)
    ple_gate_b = nrm((L, D), 0.02)
    ln2_g = gain((L, D))
    ln2_b = nrm((L, D), 0.02)
    return {"x": x, "p": p, "ln_emb_g": ln_emb_g, "ln_emb_b": ln_emb_b, "w_in": w_in,
            "rg_conv_w": rg_conv_w, "rg_conv_b": rg_conv_b, "rg_wa": rg_wa, "rg_ba": rg_ba,
            "rg_wx": rg_wx, "rg_bx": rg_bx, "rg_lambda": rg_lambda, "rg_out": rg_out,
            "gdn_conv_w": gdn_conv_w, "gdn_a_log": gdn_a_log, "gdn_dt_bias": gdn_dt_bias,
            "gdn_norm_w": gdn_norm_w, "gdn_out": gdn_out, "w_o": w_o, "ln1_g": ln1_g, "ln1_b": ln1_b,
            "peer_wq": peer_wq, "peer_keys": peer_keys, "peer_u": peer_u, "peer_v": peer_v,
            "ple_w": ple_w, "ple_gate_w": ple_gate_w, "ple_gate_b": ple_gate_b,
            "ln2_g": ln2_g, "ln2_b": ln2_b}


def reference(x, p, ln_emb_g, ln_emb_b, w_in, rg_conv_w, rg_conv_b, rg_wa, rg_ba, rg_wx, rg_bx,
              rg_lambda, rg_out, gdn_conv_w, gdn_a_log, gdn_dt_bias, gdn_norm_w, gdn_out, w_o,
              ln1_g, ln1_b, peer_wq, peer_keys, peer_u, peer_v, ple_w, ple_gate_w, ple_gate_b,
              ln2_g, ln2_b):
    h = layer_norm(x, ln_emb_g, ln_emb_b)
    for i in range(DEPTH):
        mix = token_mixer(h, w_in[i], rg_conv_w[i], rg_conv_b[i], rg_wa[i], rg_ba[i], rg_wx[i], rg_bx[i],
                          rg_lambda[i], rg_out[i], gdn_conv_w[i], gdn_a_log[i], gdn_dt_bias[i],
                          gdn_norm_w[i], gdn_out[i], w_o[i])
        x1 = layer_norm(DN_ALPHA * h + mix, ln1_g[i], ln1_b[i])
        ffn = peer(x1, peer_wq[i], peer_keys[i], peer_u[i], peer_v[i])
        ple = jax.nn.sigmoid(jnp.einsum('btd,de->bte', x1, ple_gate_w[i]) + ple_gate_b[i]) * \
            jnp.einsum('btp,pd->btd', p[i], ple_w[i])
        h = layer_norm(DN_ALPHA * x1 + ffn + ple, ln2_g[i], ln2_b[i])
    return h
```

```python
import functools
import math

import jax
import jax.numpy as jnp
from jax import lax
from jax.experimental import pallas as pl
from jax.experimental.pallas import tpu as pltpu

F32 = jnp.float32
BF16 = jnp.bfloat16
I32 = jnp.int32

LANES = 128
SUBLANES = 8
VMEM_LIMIT = 56 * 1024 * 1024

CONV_WIDTH = 4
RG_BLOCK = 128
RG_C = 8.0
GDN_HEAD_DIM = 128
GDN_CHUNK = 64
PEER_HEADS = 8
PEER_NKEYS = 128
PEER_HALF = 128
PEER_TOPK = 16
LN_EPS = 1e-5
NORM_EPS = 1e-6
HIGHEST = lax.Precision.HIGHEST
NT_DIMS = (((1,), (1,)), ((), ()))


def _cparams(*sem):
    return pltpu.CompilerParams(dimension_semantics=sem, vmem_limit_bytes=VMEM_LIMIT)


def _tile(n, target, *also_divides):
    return math.gcd(target, n, *also_divides)


def _gelu(x):
    return 0.5 * x * (1.0 + lax.erf(x * (2.0 ** -0.5)))


def _silu(x):
    return x * jax.nn.sigmoid(x)


def _add_ln_kernel(*refs, n_in, alpha, want_bf16):
    ins = refs[:n_in]
    g_ref, b_ref = refs[n_in], refs[n_in + 1]
    outs = refs[n_in + 2:]
    x = ins[0][...]
    if alpha != 1.0:
        x = alpha * x
    for r in ins[1:]:
        x = x + r[...]
    mu = jnp.mean(x, axis=-1, keepdims=True)
    xc = x - mu
    var = jnp.mean(xc * xc, axis=-1, keepdims=True)
    y = xc * lax.rsqrt(var + LN_EPS) * g_ref[...] + b_ref[...]
    outs[0][...] = y
    if want_bf16:
        outs[1][...] = y.astype(BF16)


def _add_ln(ins, g, b, alpha, want_bf16, tm=256):
    m, d = ins[0].shape
    tm = _tile(m, tm)
    row = pl.BlockSpec((tm, d), lambda i: (i, 0))
    vec = pl.BlockSpec((1, d), lambda i: (0, 0))
    out_shape = [jax.ShapeDtypeStruct((m, d), F32)]
    if want_bf16:
        out_shape.append(jax.ShapeDtypeStruct((m, d), BF16))
    return pl.pallas_call(
        functools.partial(_add_ln_kernel, n_in=len(ins), alpha=alpha, want_bf16=want_bf16),
        grid=(m // tm,),
        in_specs=[row] * len(ins) + [vec, vec],
        out_specs=[row] * len(out_shape),
        out_shape=out_shape,
        compiler_params=_cparams("parallel"),
        name="add_ln",
    )(*ins, g.reshape(1, d), b.reshape(1, d))


def _mm_kernel(a_ref, b_ref, o_ref):
    o_ref[...] = jnp.dot(a_ref[...], b_ref[...], preferred_element_type=F32).astype(o_ref.dtype)


def _matmul(a, b, tm, tn, out_dtype=F32, name="matmul"):
    m, k = a.shape
    n = b.shape[1]
    tm, tn = _tile(m, tm), _tile(n, tn)
    return pl.pallas_call(
        _mm_kernel,
        grid=(m // tm, n // tn),
        in_specs=[pl.BlockSpec((tm, k), lambda i, j: (i, 0)),
                  pl.BlockSpec((k, tn), lambda i, j: (0, j))],
        out_specs=pl.BlockSpec((tm, tn), lambda i, j: (i, j)),
        out_shape=jax.ShapeDtypeStruct((m, n), out_dtype),
        compiler_params=_cparams("parallel", "arbitrary"),
        name=name,
    )(a, b)


def _mm_nt_kernel(w_ref, a_ref, o_ref):
    o_ref[...] = lax.dot_general(w_ref[...], a_ref[...], NT_DIMS, preferred_element_type=F32)


def _matmul_nt(w, a, tm):
    r, k = w.shape
    m = a.shape[0]
    tm = _tile(m, tm)
    return pl.pallas_call(
        _mm_nt_kernel,
        grid=(m // tm,),
        in_specs=[pl.BlockSpec((r, k), lambda i: (0, 0)),
                  pl.BlockSpec((tm, k), lambda i: (i, 0))],
        out_specs=pl.BlockSpec((r, tm), lambda i: (0, i)),
        out_shape=jax.ShapeDtypeStruct((r, m), F32),
        compiler_params=_cparams("parallel"),
        name="gate_proj",
    )(w, a)


def _merge_kernel(ha_ref, ob_ref, wa_ref, wb_ref, ma_ref, mb_ref, o_ref):
    ya = jnp.dot(ha_ref[...], wa_ref[...], preferred_element_type=F32)
    yb = jnp.dot(ob_ref[...], wb_ref[...], preferred_element_type=F32)
    o_ref[...] = (jax.nn.sigmoid(ma_ref[...]) * ya + jax.nn.sigmoid(mb_ref[...]) * yb).astype(o_ref.dtype)


def _merge(ha, ob, wa, wb, proj, ma_col, mb_col, tm=512, tn=1024):
    m, k = ha.shape
    n = wa.shape[1]
    tm, tn = _tile(m, tm), _tile(n, tn, ma_col, mb_col)
    ja, jb = ma_col // tn, mb_col // tn
    return pl.pallas_call(
        _merge_kernel,
        grid=(m // tm, n // tn),
        in_specs=[pl.BlockSpec((tm, k), lambda i, j: (i, 0)),
                  pl.BlockSpec((tm, k), lambda i, j: (i, 0)),
                  pl.BlockSpec((k, tn), lambda i, j: (0, j)),
                  pl.BlockSpec((k, tn), lambda i, j: (0, j)),
                  pl.BlockSpec((tm, tn), lambda i, j: (i, ja + j)),
                  pl.BlockSpec((tm, tn), lambda i, j: (i, jb + j))],
        out_specs=pl.BlockSpec((tm, tn), lambda i, j: (i, j)),
        out_shape=jax.ShapeDtypeStruct((m, n), BF16),
        compiler_params=_cparams("parallel", "arbitrary"),
        name="merge",
    )(ha, ob, wa, wb, proj, proj)


def _ple_kernel(x_ref, wg_ref, bg_ref, p_ref, wp_ref, o_ref):
    gate = jax.nn.sigmoid(jnp.dot(x_ref[...], wg_ref[...], preferred_element_type=F32) + bg_ref[...])
    o_ref[...] = gate * jnp.dot(p_ref[...], wp_ref[...], preferred_element_type=F32)


def _ple(x1b, wg, bg, pb, wp, tm=1024, tn=512):
    m, k = x1b.shape
    n = wg.shape[1]
    kp = pb.shape[1]
    tm, tn = _tile(m, tm), _tile(n, tn)
    return pl.pallas_call(
        _ple_kernel,
        grid=(m // tm, n // tn),
        in_specs=[pl.BlockSpec((tm, k), lambda i, j: (i, 0)),
                  pl.BlockSpec((k, tn), lambda i, j: (0, j)),
                  pl.BlockSpec((1, tn), lambda i, j: (0, j)),
                  pl.BlockSpec((tm, kp), lambda i, j: (i, 0)),
                  pl.BlockSpec((kp, tn), lambda i, j: (0, j))],
        out_specs=pl.BlockSpec((tm, tn), lambda i, j: (i, j)),
        out_shape=jax.ShapeDtypeStruct((m, n), F32),
        compiler_params=_cparams("parallel", "arbitrary"),
        name="ple",
    )(x1b, wg, bg.reshape(1, n), pb, wp)


def _causal_conv(x, w, row):
    y = x * w[CONV_WIDTH - 1:CONV_WIDTH, :]
    for j in range(CONV_WIDTH - 1):
        s = CONV_WIDTH - 1 - j
        y = y + jnp.where(row >= s, pltpu.roll(x, s, 0), 0.0) * w[j:j + 1, :]
    return y


def _rglru_kernel(x_ref, y_ref, cw_ref, cb_ref, wa_ref, ba_ref, wx_ref, bx_ref, lam_ref, o_ref):
    t, c = x_ref.shape
    row = lax.broadcasted_iota(I32, (t, c), 0)
    xr = _causal_conv(x_ref[...], cw_ref[...], row) + cb_ref[...]
    xb = xr.astype(BF16)
    rs, is_ = [], []
    for k in range(c // RG_BLOCK):
        blk = xb[:, k * RG_BLOCK:(k + 1) * RG_BLOCK]
        rs.append(jnp.dot(blk, wa_ref[k], preferred_element_type=F32))
        is_.append(jnp.dot(blk, wx_ref[k], preferred_element_type=F32))
    r = jax.nn.sigmoid(jnp.concatenate(rs, axis=1) + ba_ref[...])
    i = jax.nn.sigmoid(jnp.concatenate(is_, axis=1) + bx_ref[...])
    log_a = (-RG_C) * r * jax.nn.softplus(-lam_ref[...])
    a = jnp.exp(log_a)
    b = jnp.sqrt(-jnp.tanh(log_a) * (a * a + 1.0)) * (i * xr)
    d = 1
    while d < t:
        keep = row >= d
        a_s = jnp.where(keep, pltpu.roll(a, d, 0), 1.0)
        b_s = jnp.where(keep, pltpu.roll(b, d, 0), 0.0)
        b = a * b_s + b
        a = a * a_s
        d *= 2
    o_ref[...] = (b * _gelu(y_ref[...])).astype(o_ref.dtype)


def _rglru(proj, seq, x_col, y_col, width, cw, cb, wa, ba, wx, bx, lam, ct=256):
    m = proj.shape[0]
    nb = m // seq
    ct = _tile(width, ct, x_col, y_col)
    jx, jy = x_col // ct, y_col // ct
    kb = ct // RG_BLOCK
    vec = pl.BlockSpec((1, ct), lambda b, c: (0, c))
    gate_w = pl.BlockSpec((kb, RG_BLOCK, RG_BLOCK), lambda b, c: (c, 0, 0))
    return pl.pallas_call(
        _rglru_kernel,
        grid=(nb, width // ct),
        in_specs=[pl.BlockSpec((seq, ct), lambda b, c: (b, jx + c)),
                  pl.BlockSpec((seq, ct), lambda b, c: (b, jy + c)),
                  pl.BlockSpec((CONV_WIDTH, ct), lambda b, c: (0, c)),
                  vec, gate_w, vec, gate_w, vec, vec],
        out_specs=pl.BlockSpec((seq, ct), lambda b, c: (b, c)),
        out_shape=jax.ShapeDtypeStruct((m, width), BF16),
        compiler_params=_cparams("parallel", "parallel"),
        name="rglru",
    )(proj, proj, cw, cb.reshape(1, width), wa.astype(BF16), ba.reshape(1, width),
      wx.astype(BF16), bx.reshape(1, width), lam.reshape(1, width))


def _dot_hi(a, b):
    return jnp.dot(a, b, precision=HIGHEST, preferred_element_type=F32)


def _gdn_kernel(alog_ref, dtb_ref, q_ref, k_ref, v_ref, z_ref, cwq_ref, cwk_ref, cwv_ref, gt_ref, nw_ref,
                o_ref, qs, ks, vs, gcs, bts, st, *, heads_per_step):
    t, gw = q_ref.shape
    c = GDN_CHUNK
    dh = GDN_HEAD_DIM
    nchunks = t // c
    hg = pl.program_id(1)
    nheads = gt_ref.shape[0] // 2

    row = lax.broadcasted_iota(I32, (t, gw), 0)
    q = _silu(_causal_conv(q_ref[...], cwq_ref[...], row))
    k = _silu(_causal_conv(k_ref[...], cwk_ref[...], row))
    vs[...] = _silu(_causal_conv(v_ref[...], cwv_ref[...], row))
    for g in range(heads_per_step):
        sl = slice(g * dh, (g + 1) * dh)
        qh, kh = q[:, sl], k[:, sl]
        qs[:, sl] = qh * lax.rsqrt(jnp.sum(qh * qh, axis=-1, keepdims=True) + NORM_EPS) * (dh ** -0.5)
        ks[:, sl] = kh * lax.rsqrt(jnp.sum(kh * kh, axis=-1, keepdims=True) + NORM_EPS)

    ri = lax.broadcasted_iota(I32, (c, c), 0)
    ci = lax.broadcasted_iota(I32, (c, c), 1)
    eye = ri == ci
    causal = ri >= ci
    strict = ri > ci
    eye_f = eye.astype(F32)
    upper_f = (ri <= ci).astype(F32)

    for g in range(heads_per_step):
        h = hg * heads_per_step + g
        a_gate = gt_ref[nheads + h]
        b_gate = gt_ref[h]
        neg_rate = -jnp.exp(jnp.zeros_like(a_gate) + alog_ref[h])
        g_log = neg_rate * jax.nn.softplus(a_gate + dtb_ref[h])
        gcs[g] = _dot_hi(g_log, upper_f)
        bts[g] = jax.nn.sigmoid(b_gate)
        st[g] = jnp.zeros((dh, dh), F32)

    def to_col(row_vec):
        return jnp.sum(jnp.where(eye, jnp.broadcast_to(row_vec, (c, c)), 0.0), axis=1, keepdims=True)

    def chunk_step(n, carry):
        r0 = pl.multiple_of(n * c, c)
        for g in range(heads_per_step):
            sl = slice(g * dh, (g + 1) * dh)
            gc_row = gcs[g, pl.ds(n, 1), :]
            gc_col = to_col(gc_row)
            beta_col = to_col(bts[g, pl.ds(n, 1), :])
            g_last = gc_row[:, c - 1:c]
            decay = jnp.where(causal, jnp.exp(jnp.where(causal, gc_col - gc_row, 0.0)), 0.0)
            e_col = jnp.exp(gc_col)

            qc = qs[pl.ds(r0, c), sl]
            kc = ks[pl.ds(r0, c), sl]
            vc = vs[pl.ds(r0, c), sl]
            kb = kc * beta_col
            a_low = jnp.where(strict, lax.dot_general(kb, kc, NT_DIMS, precision=HIGHEST,
                                                      preferred_element_type=F32) * decay, 0.0)
            inv = eye_f - a_low
            pw = _dot_hi(a_low, a_low)
            span = 2
            while True:
                inv = inv + _dot_hi(inv, pw)
                span *= 2
                if span >= c:
                    break
                pw = _dot_hi(pw, pw)
            rhs = jnp.concatenate([vc * beta_col, kb * e_col], axis=1)
            sol = _dot_hi(inv, rhs)
            u_c, w_c = sol[:, :dh], sol[:, dh:]

            attn = lax.dot_general(qc, kc, NT_DIMS, preferred_element_type=F32) * decay
            state = st[g]
            v_new = u_c - jnp.dot(w_c, state, preferred_element_type=F32)
            o_c = (jnp.dot(qc * e_col, state, preferred_element_type=F32)
                   + jnp.dot(attn, v_new, preferred_element_type=F32))
            k_d = kc * jnp.exp(g_last - gc_col)
            st[g] = state * jnp.exp(g_last) + jnp.dot(k_d.T, v_new, preferred_element_type=F32)

            o_n = o_c * lax.rsqrt(jnp.mean(o_c * o_c, axis=-1, keepdims=True) + NORM_EPS)
            o_ref[pl.ds(r0, c), sl] = (o_n * nw_ref[...] * _silu(z_ref[pl.ds(r0, c), sl])).astype(o_ref.dtype)
        return carry

    lax.fori_loop(0, nchunks, chunk_step, 0)


def _gdn(proj, gates_t, seq, q_col, k_col, v_col, z_col, width, conv_w, a_log, dt_bias, norm_w,
         heads_per_step=2):
    m = proj.shape[0]
    nb = m // seq
    gw = heads_per_step * GDN_HEAD_DIM
    nh = width // GDN_HEAD_DIM
    nchunks = seq // GDN_CHUNK
    jq, jk, jv, jz = (col // gw for col in (q_col, k_col, v_col, z_col))
    wblocks = width // gw
    gt3 = gates_t.reshape(2 * nh, m // GDN_CHUNK, GDN_CHUNK)
    smem = pl.BlockSpec(memory_space=pltpu.SMEM)

    def col_spec(j0):
        return pl.BlockSpec((seq, gw), lambda b, h: (b, j0 + h))

    def cw_spec(j0):
        return pl.BlockSpec((CONV_WIDTH, gw), lambda b, h: (0, j0 + h))

    return pl.pallas_call(
        functools.partial(_gdn_kernel, heads_per_step=heads_per_step),
        grid=(nb, nh // heads_per_step),
        in_specs=[smem, smem, col_spec(jq), col_spec(jk), col_spec(jv), col_spec(jz),
                  cw_spec(0), cw_spec(wblocks), cw_spec(2 * wblocks),
                  pl.BlockSpec((2 * nh, nchunks, GDN_CHUNK), lambda b, h: (0, b, 0)),
                  pl.BlockSpec((1, GDN_HEAD_DIM), lambda b, h: (0, 0))],
        out_specs=pl.BlockSpec((seq, gw), lambda b, h: (b, h)),
        out_shape=jax.ShapeDtypeStruct((m, width), BF16),
        scratch_shapes=[pltpu.VMEM((seq, gw), F32), pltpu.VMEM((seq, gw), F32), pltpu.VMEM((seq, gw), F32),
                        pltpu.VMEM((heads_per_step, nchunks, GDN_CHUNK), F32),
                        pltpu.VMEM((heads_per_step, nchunks, GDN_CHUNK), F32),
                        pltpu.VMEM((heads_per_step, GDN_HEAD_DIM, GDN_HEAD_DIM), F32)],
        compiler_params=_cparams("parallel", "parallel"),
        name="gdn",
    )(a_log, dt_bias, proj, proj, proj, proj, conv_w, conv_w, conv_w, gt3, norm_w.reshape(1, GDN_HEAD_DIM))


def _top_k_rows(s, k, payload=None):
    r = s.shape[0]
    rowi = lax.broadcasted_iota(I32, s.shape, 0)
    vals, picks = [], []
    for _ in range(k):
        mx = jnp.max(s, axis=0, keepdims=True)
        arg = jnp.min(jnp.where(s == mx, rowi, r), axis=0, keepdims=True)
        hit = rowi == arg
        vals.append(mx)
        if payload is None:
            picks.append(arg)
        else:
            picks.append(jnp.max(jnp.where(hit, payload, -1), axis=0, keepdims=True))
        s = jnp.where(hit, -jnp.inf, s)
    return jnp.concatenate(vals, axis=0), jnp.concatenate(picks, axis=0)


def _route_kernel(q_ref, keys_ref, o_ref, i1s, i2s, gts, gbuf):
    tm = q_ref.shape[0]
    kk = PEER_TOPK
    i1_rows, i2_rows, gate_rows = [], [], []
    for h in range(PEER_HEADS):
        tops = []
        for p in range(2):
            c0 = (h * 2 + p) * PEER_HALF
            qhp = q_ref[:, c0:c0 + PEER_HALF].astype(BF16)
            s_t = lax.dot_general(keys_ref[h, p], qhp, NT_DIMS, preferred_element_type=F32)
            tops.append(_top_k_rows(s_t, kk))
        (s1, i1), (s2, i2) = tops
        cand = jnp.concatenate([s1[a:a + 1, :] + s2 for a in range(kk)], axis=0)
        cand_id = jnp.concatenate([i1[a:a + 1, :] * PEER_NKEYS + i2 for a in range(kk)], axis=0)
        best, ids = _top_k_rows(cand, kk, payload=cand_id)
        e = jnp.exp(best - best[0:1, :])
        gate_rows.append(e / jnp.sum(e, axis=0, keepdims=True))
        i1_rows.append(jnp.right_shift(ids, PEER_NKEYS.bit_length() - 1))
        i2_rows.append(jnp.bitwise_and(ids, PEER_NKEYS - 1))
    i1s[...] = jnp.concatenate(i1_rows, axis=0).T
    i2s[...] = jnp.concatenate(i2_rows, axis=0).T
    gts[...] = jnp.concatenate(gate_rows, axis=0).T

    nslots = PEER_HEADS * kk
    key_iota = lax.broadcasted_iota(I32, (PEER_NKEYS, nslots), 0)

    def group(gi, carry):
        base = pl.multiple_of(gi * SUBLANES, SUBLANES)
        for tk in range(SUBLANES):
            i1r = jnp.broadcast_to(i1s[pl.ds(base + tk, 1), :], (PEER_NKEYS, nslots))
            i2r = jnp.broadcast_to(i2s[pl.ds(base + tk, 1), :], (PEER_NKEYS, nslots))
            gr = jnp.broadcast_to(gts[pl.ds(base + tk, 1), :], (PEER_NKEYS, nslots))
            a_t = jnp.where(key_iota == i1r, gr, 0.0)
            a_hi = a_t.astype(BF16)
            a_lo = (a_t - a_hi.astype(F32)).astype(BF16)
            b_t = jnp.where(key_iota == i2r, 1.0, 0.0).astype(BF16)
            lhs = jnp.concatenate([a_hi, a_lo], axis=1)
            rhs = jnp.concatenate([b_t, b_t], axis=1)
            gbuf[pl.ds(tk * PEER_NKEYS, PEER_NKEYS), :] = lax.dot_general(
                lhs, rhs, NT_DIMS, preferred_element_type=F32)
        for i1 in range(PEER_NKEYS):
            o_ref[gi, i1] = gbuf[pl.ds(i1, SUBLANES, stride=PEER_NKEYS), :]
        return carry

    lax.fori_loop(0, tm // SUBLANES, group, 0)


def _route(q, keys, tm=128):
    m, qd = q.shape
    tm = _tile(m, tm)
    nslots = PEER_HEADS * PEER_TOPK
    return pl.pallas_call(
        _route_kernel,
        grid=(m // tm,),
        in_specs=[pl.BlockSpec((tm, qd), lambda i: (i, 0)),
                  pl.BlockSpec(keys.shape, lambda i: (0, 0, 0, 0))],
        out_specs=pl.BlockSpec((tm // SUBLANES, PEER_NKEYS, SUBLANES, PEER_NKEYS), lambda i: (i, 0, 0, 0)),
        out_shape=jax.ShapeDtypeStruct((m // SUBLANES, PEER_NKEYS, SUBLANES, PEER_NKEYS), F32),
        scratch_shapes=[pltpu.VMEM((tm, nslots), I32), pltpu.VMEM((tm, nslots), I32),
                        pltpu.VMEM((tm, nslots), F32),
                        pltpu.VMEM((SUBLANES * PEER_NKEYS, PEER_NKEYS), F32)],
        compiler_params=_cparams("parallel"),
        name="peer_route",
    )(q, keys)


def _peer_kernel(x_ref, u_ref, v_ref, g_ref, o_ref, *, i1_per_step):
    j = pl.program_id(1)
    tm = x_ref.shape[0]
    hid = lax.dot_general(x_ref[...], u_ref[...], NT_DIMS, preferred_element_type=F32)
    parts = []
    for gi in range(i1_per_step):
        gate = g_ref[:, gi].reshape(tm, PEER_NKEYS)
        parts.append((gate * _gelu(hid[:, gi * PEER_NKEYS:(gi + 1) * PEER_NKEYS])).astype(BF16))
    w = jnp.concatenate(parts, axis=1)
    contrib = jnp.dot(w, v_ref[...], preferred_element_type=F32)

    @pl.when(j == 0)
    def _():
        o_ref[...] = contrib

    @pl.when(j > 0)
    def _():
        o_ref[...] += contrib


def _peer(x1b, ub, vb, gates, tm=512, i1_per_step=4):
    m, d = x1b.shape
    ne = ub.shape[0]
    tm = _tile(m, tm)
    te = i1_per_step * PEER_NKEYS
    return pl.pallas_call(
        functools.partial(_peer_kernel, i1_per_step=i1_per_step),
        grid=(m // tm, ne // te),
        in_specs=[pl.BlockSpec((tm, d), lambda i, j: (i, 0)),
                  pl.BlockSpec((te, d), lambda i, j: (j, 0)),
                  pl.BlockSpec((te, d), lambda i, j: (j, 0)),
                  pl.BlockSpec((tm // SUBLANES, i1_per_step, SUBLANES, PEER_NKEYS), lambda i, j: (i, j, 0, 0))],
        out_specs=pl.BlockSpec((tm, d), lambda i, j: (i, 0)),
        out_shape=jax.ShapeDtypeStruct((m, d), F32),
        compiler_params=_cparams("parallel", "arbitrary"),
        name="peer_experts",
    )(x1b, ub, vb, gates)


def _layer(h, hb, p, seq, w_in, rg_conv_w, rg_conv_b, rg_wa, rg_ba, rg_wx, rg_bx, rg_lambda, rg_out,
           gdn_conv_w, gdn_a_log, gdn_dt_bias, gdn_norm_w, gdn_out, w_o, ln1_g, ln1_b, peer_wq, peer_keys,
           peer_u, peer_v, ple_w, ple_gate_w, ple_gate_b, ln2_g, ln2_b, alpha, last):
    d = h.shape[1]
    rg_w = rg_out.shape[0]
    gdn_w = gdn_out.shape[0]
    nh = gdn_w // GDN_HEAD_DIM
    n_main = 2 * rg_w + 4 * gdn_w
    g0 = n_main
    w_main = jnp.concatenate([w_in[:, :g0], w_in[:, g0 + 2 * nh:]], axis=1).astype(BF16)
    w_gate_t = w_in[:, g0:g0 + 2 * nh].T.astype(BF16)
    cols = {"rg_x": 0, "rg_y": rg_w, "q": 2 * rg_w, "k": 2 * rg_w + gdn_w, "v": 2 * rg_w + 2 * gdn_w,
            "z": 2 * rg_w + 3 * gdn_w, "m_a": n_main, "m_b": n_main + d}

    proj = _matmul(hb, w_main, tm=1024, tn=1024, name="in_proj")
    gates_t = _matmul_nt(w_gate_t, hb, tm=512)

    h_a = _rglru(proj, seq, cols["rg_x"], cols["rg_y"], rg_w, rg_conv_w, rg_conv_b, rg_wa, rg_ba,
                 rg_wx, rg_bx, rg_lambda)
    o_b = _gdn(proj, gates_t, seq, cols["q"], cols["k"], cols["v"], cols["z"], gdn_w, gdn_conv_w,
               gdn_a_log, gdn_dt_bias, gdn_norm_w)
    merged = _merge(h_a, o_b, rg_out.astype(BF16), gdn_out.astype(BF16), proj, cols["m_a"], cols["m_b"])
    mix = _matmul(merged, w_o.astype(BF16), tm=1024, tn=1024, name="out_proj")
    x1, x1b = _add_ln([h, mix], ln1_g, ln1_b, alpha, want_bf16=True)

    q = _matmul(x1b, peer_wq.astype(BF16), tm=1024, tn=1024, name="peer_query")
    gates = _route(q, peer_keys.astype(BF16))
    ffn = _peer(x1b, peer_u.astype(BF16), peer_v.astype(BF16), gates)
    ple = _ple(x1b, ple_gate_w.astype(BF16), ple_gate_b, p.astype(BF16), ple_w.astype(BF16))
    out = _add_ln([x1, ffn, ple], ln2_g, ln2_b, alpha, want_bf16=not last)
    return (out[0], None) if last else tuple(out)


def kernel(x, p, ln_emb_g, ln_emb_b, w_in, rg_conv_w, rg_conv_b, rg_wa, rg_ba, rg_wx, rg_bx, rg_lambda, rg_out, gdn_conv_w, gdn_a_log, gdn_dt_bias, gdn_norm_w, gdn_out, w_o, ln1_g, ln1_b, peer_wq, peer_keys, peer_u, peer_v, ple_w, ple_gate_w, ple_gate_b, ln2_g, ln2_b):
    bsz, seq, d = x.shape
    depth = w_in.shape[0]
    m = bsz * seq
    alpha = (2.0 * depth) ** 0.25
    h, hb = _add_ln([x.reshape(m, d)], ln_emb_g, ln_emb_b, 1.0, want_bf16=True)
    for i in range(depth):
        h, hb = _layer(h, hb, p[i].reshape(m, -1), seq, w_in[i], rg_conv_w[i], rg_conv_b[i], rg_wa[i], rg_ba[i],
                       rg_wx[i], rg_bx[i], rg_lambda[i], rg_out[i], gdn_conv_w[i], gdn_a_log[i], gdn_dt_bias[i],
                       gdn_norm_w[i], gdn_out[i], w_o[i], ln1_g[i], ln1_b[i], peer_wq[i], peer_keys[i],
                       peer_u[i], peer_v[i], ple_w[i], ple_gate_w[i], ple_gate_b[i], ln2_g[i], ln2_b[i], alpha, i == depth - 1)
    return h.reshape(bsz, seq, d)
```

```python
import functools
import math

import jax
import jax.numpy as jnp
from jax import lax
from jax.experimental import pallas as pl
from jax.experimental.pallas import tpu as pltpu

F32 = jnp.float32
BF16 = jnp.bfloat16
I32 = jnp.int32

LANES = 128
SUBLANES = 8
VMEM_LIMIT = 56 * 1024 * 1024

CONV_WIDTH = 4
RG_BLOCK = 128
RG_C = 8.0
GDN_HEAD_DIM = 128
GDN_CHUNK = 64
PEER_HEADS = 8
PEER_NKEYS = 128
PEER_HALF = 128
PEER_TOPK = 16
LN_EPS = 1e-5
NORM_EPS = 1e-6
HIGHEST = lax.Precision.HIGHEST
NT_DIMS = (((1,), (1,)), ((), ()))


def _cparams(*sem):
    return pltpu.CompilerParams(dimension_semantics=sem, vmem_limit_bytes=VMEM_LIMIT)


def _tile(n, target, *also_divides):
    return math.gcd(target, n, *also_divides)


def _gelu(x):
    return 0.5 * x * (1.0 + lax.erf(x * (2.0 ** -0.5)))


def _silu(x):
    return x * jax.nn.sigmoid(x)


def _add_ln_kernel(*refs, n_in, alpha, want_bf16):
    ins = refs[:n_in]
    g_ref, b_ref = refs[n_in], refs[n_in + 1]
    outs = refs[n_in + 2:]
    x = ins[0][...]
    if alpha != 1.0:
        x = alpha * x
    for r in ins[1:]:
        x = x + r[...]
    mu = jnp.mean(x, axis=-1, keepdims=True)
    xc = x - mu
    var = jnp.mean(xc * xc, axis=-1, keepdims=True)
    y = xc * lax.rsqrt(var + LN_EPS) * g_ref[...] + b_ref[...]
    outs[0][...] = y
    if want_bf16:
        outs[1][...] = y.astype(BF16)


def _add_ln(ins, g, b, alpha, want_bf16, tm=256):
    m, d = ins[0].shape
    tm = _tile(m, tm)
    row = pl.BlockSpec((tm, d), lambda i: (i, 0))
    vec = pl.BlockSpec((1, d), lambda i: (0, 0))
    out_shape = [jax.ShapeDtypeStruct((m, d), F32)]
    if want_bf16:
        out_shape.append(jax.ShapeDtypeStruct((m, d), BF16))
    return pl.pallas_call(
        functools.partial(_add_ln_kernel, n_in=len(ins), alpha=alpha, want_bf16=want_bf16),
        grid=(m // tm,),
        in_specs=[row] * len(ins) + [vec, vec],
        out_specs=[row] * len(out_shape),
        out_shape=out_shape,
        compiler_params=_cparams("parallel"),
        name="add_ln",
    )(*ins, g.reshape(1, d), b.reshape(1, d))


def _mm_kernel(a_ref, b_ref, o_ref):
    o_ref[...] = jnp.dot(a_ref[...], b_ref[...].astype(BF16), preferred_element_type=F32).astype(o_ref.dtype)


def _matmul(a, b, tm, tn, n=None, out_dtype=F32, name="matmul"):
    m, k = a.shape
    n = b.shape[1] if n is None else n
    tm, tn = _tile(m, tm), _tile(n, tn)
    return pl.pallas_call(
        _mm_kernel,
        grid=(m // tm, n // tn),
        in_specs=[pl.BlockSpec((tm, k), lambda i, j: (i, 0)),
                  pl.BlockSpec((k, tn), lambda i, j: (0, j))],
        out_specs=pl.BlockSpec((tm, tn), lambda i, j: (i, j)),
        out_shape=jax.ShapeDtypeStruct((m, n), out_dtype),
        compiler_params=_cparams("parallel", "arbitrary"),
        name=name,
    )(a, b)


def _mm_nt_kernel(w_ref, a_ref, o_ref):
    o_ref[...] = lax.dot_general(w_ref[...], a_ref[...], NT_DIMS, preferred_element_type=F32)


def _matmul_nt(w, a, tm):
    r, k = w.shape
    m = a.shape[0]
    tm = _tile(m, tm)
    return pl.pallas_call(
        _mm_nt_kernel,
        grid=(m // tm,),
        in_specs=[pl.BlockSpec((r, k), lambda i: (0, 0)),
                  pl.BlockSpec((tm, k), lambda i: (i, 0))],
        out_specs=pl.BlockSpec((r, tm), lambda i: (0, i)),
        out_shape=jax.ShapeDtypeStruct((r, m), F32),
        compiler_params=_cparams("parallel"),
        name="gate_proj",
    )(w, a)


def _merge_kernel(ha_ref, ob_ref, wa_ref, wb_ref, ma_ref, mb_ref, o_ref):
    ya = jnp.dot(ha_ref[...], wa_ref[...].astype(BF16), preferred_element_type=F32)
    yb = jnp.dot(ob_ref[...], wb_ref[...].astype(BF16), preferred_element_type=F32)
    o_ref[...] = (jax.nn.sigmoid(ma_ref[...]) * ya + jax.nn.sigmoid(mb_ref[...]) * yb).astype(o_ref.dtype)


def _merge(ha, ob, wa, wb, proj, ma_col, mb_col, tm=1024, tn=512):
    m, k = ha.shape
    n = wa.shape[1]
    tm, tn = _tile(m, tm), _tile(n, tn, ma_col, mb_col)
    ja, jb = ma_col // tn, mb_col // tn
    return pl.pallas_call(
        _merge_kernel,
        grid=(m // tm, n // tn),
        in_specs=[pl.BlockSpec((tm, k), lambda i, j: (i, 0)),
                  pl.BlockSpec((tm, k), lambda i, j: (i, 0)),
                  pl.BlockSpec((k, tn), lambda i, j: (0, j)),
                  pl.BlockSpec((k, tn), lambda i, j: (0, j)),
                  pl.BlockSpec((tm, tn), lambda i, j: (i, ja + j)),
                  pl.BlockSpec((tm, tn), lambda i, j: (i, jb + j))],
        out_specs=pl.BlockSpec((tm, tn), lambda i, j: (i, j)),
        out_shape=jax.ShapeDtypeStruct((m, n), BF16),
        compiler_params=_cparams("parallel", "arbitrary"),
        name="merge",
    )(ha, ob, wa, wb, proj, proj)


def _ple_kernel(x_ref, wg_ref, bg_ref, p_ref, wp_ref, o_ref):
    gate = jax.nn.sigmoid(jnp.dot(x_ref[...], wg_ref[...].astype(BF16), preferred_element_type=F32) + bg_ref[...])
    o_ref[...] = gate * jnp.dot(p_ref[...].astype(BF16), wp_ref[...].astype(BF16), preferred_element_type=F32)


def _ple(x1b, wg, bg, pb, wp, tm=1024, tn=512):
    m, k = x1b.shape
    n = wg.shape[1]
    kp = pb.shape[1]
    tm, tn = _tile(m, tm), _tile(n, tn)
    return pl.pallas_call(
        _ple_kernel,
        grid=(m // tm, n // tn),
        in_specs=[pl.BlockSpec((tm, k), lambda i, j: (i, 0)),
                  pl.BlockSpec((k, tn), lambda i, j: (0, j)),
                  pl.BlockSpec((1, tn), lambda i, j: (0, j)),
                  pl.BlockSpec((tm, kp), lambda i, j: (i, 0)),
                  pl.BlockSpec((kp, tn), lambda i, j: (0, j))],
        out_specs=pl.BlockSpec((tm, tn), lambda i, j: (i, j)),
        out_shape=jax.ShapeDtypeStruct((m, n), F32),
        compiler_params=_cparams("parallel", "arbitrary"),
        name="ple",
    )(x1b, wg, bg.reshape(1, n), pb, wp)


def _causal_conv(x, w, row):
    y = x * w[CONV_WIDTH - 1:CONV_WIDTH, :]
    for j in range(CONV_WIDTH - 1):
        s = CONV_WIDTH - 1 - j
        y = y + jnp.where(row >= s, pltpu.roll(x, s, 0), 0.0) * w[j:j + 1, :]
    return y


def _rglru_kernel(x_ref, y_ref, cw_ref, cb_ref, wa_ref, ba_ref, wx_ref, bx_ref, lam_ref, o_ref):
    t, c = x_ref.shape
    row = lax.broadcasted_iota(I32, (t, c), 0)
    xr = _causal_conv(x_ref[...], cw_ref[...], row) + cb_ref[...]
    xb = xr.astype(BF16)
    rs, is_ = [], []
    for k in range(c // RG_BLOCK):
        blk = xb[:, k * RG_BLOCK:(k + 1) * RG_BLOCK]
        rs.append(jnp.dot(blk, wa_ref[k], preferred_element_type=F32))
        is_.append(jnp.dot(blk, wx_ref[k], preferred_element_type=F32))
    r = jax.nn.sigmoid(jnp.concatenate(rs, axis=1) + ba_ref[...])
    i = jax.nn.sigmoid(jnp.concatenate(is_, axis=1) + bx_ref[...])
    log_a = (-RG_C) * r * jax.nn.softplus(-lam_ref[...])
    a = jnp.exp(log_a)
    b = jnp.sqrt(-jnp.tanh(log_a) * (a * a + 1.0)) * (i * xr)
    d = 1
    while d < t:
        keep = row >= d
        a_s = jnp.where(keep, pltpu.roll(a, d, 0), 1.0)
        b_s = jnp.where(keep, pltpu.roll(b, d, 0), 0.0)
        b = a * b_s + b
        a = a * a_s
        d *= 2
    o_ref[...] = (b * _gelu(y_ref[...])).astype(o_ref.dtype)


def _rglru(proj, seq, x_col, y_col, width, cw, cb, wa, ba, wx, bx, lam, ct=256):
    m = proj.shape[0]
    nb = m // seq
    ct = _tile(width, ct, x_col, y_col)
    jx, jy = x_col // ct, y_col // ct
    kb = ct // RG_BLOCK
    vec = pl.BlockSpec((1, ct), lambda b, c: (0, c))
    gate_w = pl.BlockSpec((kb, RG_BLOCK, RG_BLOCK), lambda b, c: (c, 0, 0))
    return pl.pallas_call(
        _rglru_kernel,
        grid=(nb, width // ct),
        in_specs=[pl.BlockSpec((seq, ct), lambda b, c: (b, jx + c)),
                  pl.BlockSpec((seq, ct), lambda b, c: (b, jy + c)),
                  pl.BlockSpec((CONV_WIDTH, ct), lambda b, c: (0, c)),
                  vec, gate_w, vec, gate_w, vec, vec],
        out_specs=pl.BlockSpec((seq, ct), lambda b, c: (b, c)),
        out_shape=jax.ShapeDtypeStruct((m, width), BF16),
        compiler_params=_cparams("parallel", "parallel"),
        name="rglru",
    )(proj, proj, cw, cb.reshape(1, width), wa.astype(BF16), ba.reshape(1, width),
      wx.astype(BF16), bx.reshape(1, width), lam.reshape(1, width))


def _dot_hi(a, b):
    return jnp.dot(a, b, precision=HIGHEST, preferred_element_type=F32)


def _split_bf16(x):
    hi = x.astype(BF16)
    return hi, x - hi.astype(F32)


def _lhs3(x_dup, first):
    _, lo = _split_bf16(x_dup)
    return jnp.concatenate([jnp.where(first, x_dup, lo).astype(BF16),
                            jnp.where(first, x_dup, 0.0).astype(BF16)], axis=1)


def _rhs3(p):
    hi, lo = _split_bf16(p)
    return jnp.concatenate([hi, hi, lo.astype(BF16), jnp.zeros_like(hi)], axis=0)


def _dot3(lhs3, rhs3):
    return jnp.dot(lhs3, rhs3, preferred_element_type=F32)


def _gdn_kernel(alog_ref, dtb_ref, q_ref, k_ref, v_ref, z_ref, cwq_ref, cwk_ref, cwv_ref, gt_ref, nw_ref,
                o_ref, qs, ks, vs, us, ws, qgs, at, kdt, gcs, bts, st, *, heads_per_step, chunks_per_iter):
    t, gw = q_ref.shape
    c = GDN_CHUNK
    dh = GDN_HEAD_DIM
    nchunks = t // c
    hg = pl.program_id(1)
    nheads = gt_ref.shape[0] // 2

    row = lax.broadcasted_iota(I32, (t, gw), 0)
    q = _silu(_causal_conv(q_ref[...], cwq_ref[...], row))
    k = _silu(_causal_conv(k_ref[...], cwk_ref[...], row))
    vs[...] = _silu(_causal_conv(v_ref[...], cwv_ref[...], row))
    for g in range(heads_per_step):
        sl = slice(g * dh, (g + 1) * dh)
        qh, kh = q[:, sl], k[:, sl]
        qs[:, sl] = qh * lax.rsqrt(jnp.sum(qh * qh, axis=-1, keepdims=True) + NORM_EPS) * (dh ** -0.5)
        ks[:, sl] = kh * lax.rsqrt(jnp.sum(kh * kh, axis=-1, keepdims=True) + NORM_EPS)

    ri = lax.broadcasted_iota(I32, (c, 2 * c), 0)
    lane = lax.broadcasted_iota(I32, (c, 2 * c), 1)
    ci = jnp.bitwise_and(lane, c - 1)
    first = lane < c
    first2 = lax.broadcasted_iota(I32, (2 * c, 2 * c), 1) < c
    eye_first = ri == lane
    causal = ri >= ci
    strict = ri > ci
    eye_f = (ri == ci).astype(F32)
    upper_f = (ri <= ci).astype(F32)

    for g in range(heads_per_step):
        h = hg * heads_per_step + g
        a_gate = gt_ref[nheads + h]
        b_gate = gt_ref[h]
        neg_rate = -jnp.exp(jnp.zeros_like(a_gate) + alog_ref[h])
        g_log = neg_rate * jax.nn.softplus(a_gate + dtb_ref[h])
        gcs[g] = _dot_hi(g_log, upper_f)
        bts[g] = _dot_hi(jax.nn.sigmoid(b_gate), eye_f)
        st[g] = jnp.zeros((dh, dh), F32)

    def to_col(row_vec):
        return jnp.sum(jnp.where(eye_first, jnp.broadcast_to(row_vec, (c, 2 * c)), 0.0), axis=1, keepdims=True)

    def prepare_step(i, carry):
        chains = [(i * chunks_per_iter + j, g) for j in range(chunks_per_iter) for g in range(heads_per_step)]
        rows = [pl.ds(pl.multiple_of(n * c, c), c) for n, _ in chains]
        cols = [slice(g * dh, (g + 1) * dh) for _, g in chains]
        every = range(len(chains))
        gc_row = [gcs[g, pl.ds(n, 1), :] for n, g in chains]
        gc_col = [to_col(r) for r in gc_row]
        beta_col = [to_col(bts[g, pl.ds(n, 1), :]) for n, g in chains]
        decay = [jnp.where(causal, jnp.exp(jnp.where(causal, gc_col[x] - gc_row[x], 0.0)), 0.0) for x in every]
        e_col = [jnp.exp(cv) for cv in gc_col]
        qc = [qs[rows[x], cols[x]] for x in every]
        kc = [ks[rows[x], cols[x]] for x in every]
        vc = [vs[rows[x], cols[x]] for x in every]
        kb = [kc[x] * beta_col[x] for x in every]

        kk = []
        k2_hi = []
        for x in every:
            kb_hi, kb_lo = _split_bf16(kb[x])
            hi, lo = _split_bf16(jnp.concatenate([kc[x], kc[x]], axis=0))
            k2_hi.append(hi)
            kk.append(lax.dot_general(jnp.concatenate([kb_hi, kb_lo.astype(BF16), kb_hi], axis=1),
                                      jnp.concatenate([hi, hi, lo.astype(BF16)], axis=1),
                                      NT_DIMS, preferred_element_type=F32))
        a_low = [jnp.where(strict, kk[x] * decay[x], 0.0) for x in every]
        inv = [eye_f - a for a in a_low]
        pw = [_dot3(_lhs3(a, first), _rhs3(a)) for a in a_low]
        span = 2
        while 2 * span < c:
            both = [_dot3(_lhs3(jnp.concatenate([inv[x], pw[x]], axis=0), first2), _rhs3(pw[x])) for x in every]
            inv = [inv[x] + both[x][:c] for x in every]
            pw = [b[c:] for b in both]
            span *= 2
        inv = [inv[x] + _dot3(_lhs3(inv[x], first), _rhs3(pw[x])) for x in every]
        sol = [_dot3(_lhs3(inv[x], first),
                     _rhs3(jnp.concatenate([vc[x] * beta_col[x], kb[x] * e_col[x]], axis=1))) for x in every]
        attn = [lax.dot_general(qc[x].astype(BF16), k2_hi[x], NT_DIMS, preferred_element_type=F32) * decay[x]
                for x in every]
        for x, (n, g) in enumerate(chains):
            us[rows[x], cols[x]] = sol[x][:, :dh]
            ws[rows[x], cols[x]] = sol[x][:, dh:].astype(BF16)
            at[g, rows[x], :] = attn[x].astype(BF16)
            qgs[rows[x], cols[x]] = (qc[x] * e_col[x]).astype(BF16)
            kdt[g, n] = (kc[x] * jnp.exp(gc_row[x][:, c - 1:c] - gc_col[x])).T.astype(BF16)
        return carry

    lax.fori_loop(0, nchunks // chunks_per_iter, prepare_step, 0)

    def state_step(n, carry):
        rows = pl.ds(pl.multiple_of(n * c, c), c)
        heads = range(heads_per_step)
        cols = [slice(g * dh, (g + 1) * dh) for g in heads]
        state = [st[g] for g in heads]
        state_b = [s.astype(BF16) for s in state]
        decay_last = [jnp.exp(gcs[g, pl.ds(n, 1), :][:, c - 1:c]) for g in heads]
        v_new = [us[rows, cols[g]] - jnp.dot(ws[rows, cols[g]], state_b[g], preferred_element_type=F32)
                 for g in heads]
        v_new_b = [v.astype(BF16) for v in v_new]
        o_state = [jnp.dot(qgs[rows, cols[g]], state_b[g], preferred_element_type=F32) for g in heads]
        o_c = [o_state[g] + jnp.dot(at[g, rows, :][:, :c], v_new_b[g], preferred_element_type=F32) for g in heads]
        for g in heads:
            st[g] = state[g] * decay_last[g] + jnp.dot(kdt[g, n], v_new_b[g], preferred_element_type=F32)
        for g in heads:
            o_n = o_c[g] * lax.rsqrt(jnp.mean(o_c[g] * o_c[g], axis=-1, keepdims=True) + NORM_EPS)
            o_ref[rows, cols[g]] = (o_n * nw_ref[...] * _silu(z_ref[rows, cols[g]])).astype(o_ref.dtype)
        return carry

    lax.fori_loop(0, nchunks, state_step, 0)


def _gdn(proj, gates_t, seq, q_col, k_col, v_col, z_col, width, conv_w, a_log, dt_bias, norm_w,
         heads_per_step=2, chunks_per_iter=4):
    m = proj.shape[0]
    nb = m // seq
    gw = heads_per_step * GDN_HEAD_DIM
    nh = width // GDN_HEAD_DIM
    nchunks = seq // GDN_CHUNK
    jq, jk, jv, jz = (col // gw for col in (q_col, k_col, v_col, z_col))
    wblocks = width // gw
    gt3 = gates_t.reshape(2 * nh, m // GDN_CHUNK, GDN_CHUNK)
    smem = pl.BlockSpec(memory_space=pltpu.SMEM)

    def col_spec(j0):
        return pl.BlockSpec((seq, gw), lambda b, h: (b, j0 + h))

    def cw_spec(j0):
        return pl.BlockSpec((CONV_WIDTH, gw), lambda b, h: (0, j0 + h))

    return pl.pallas_call(
        functools.partial(_gdn_kernel, heads_per_step=heads_per_step, chunks_per_iter=chunks_per_iter),
        grid=(nb, nh // heads_per_step),
        in_specs=[smem, smem, col_spec(jq), col_spec(jk), col_spec(jv), col_spec(jz),
                  cw_spec(0), cw_spec(wblocks), cw_spec(2 * wblocks),
                  pl.BlockSpec((2 * nh, nchunks, GDN_CHUNK), lambda b, h: (0, b, 0)),
                  pl.BlockSpec((1, GDN_HEAD_DIM), lambda b, h: (0, 0))],
        out_specs=pl.BlockSpec((seq, gw), lambda b, h: (b, h)),
        out_shape=jax.ShapeDtypeStruct((m, width), BF16),
        scratch_shapes=[pltpu.VMEM((seq, gw), F32), pltpu.VMEM((seq, gw), F32), pltpu.VMEM((seq, gw), F32),
                        pltpu.VMEM((seq, gw), F32), pltpu.VMEM((seq, gw), BF16), pltpu.VMEM((seq, gw), BF16),
                        pltpu.VMEM((heads_per_step, seq, 2 * GDN_CHUNK), BF16),
                        pltpu.VMEM((heads_per_step, nchunks, GDN_HEAD_DIM, GDN_CHUNK), BF16),
                        pltpu.VMEM((heads_per_step, nchunks, 2 * GDN_CHUNK), F32),
                        pltpu.VMEM((heads_per_step, nchunks, 2 * GDN_CHUNK), F32),
                        pltpu.VMEM((heads_per_step, GDN_HEAD_DIM, GDN_HEAD_DIM), F32)],
        compiler_params=_cparams("parallel", "parallel"),
        name="gdn",
    )(a_log, dt_bias, proj, proj, proj, proj, conv_w, conv_w, conv_w, gt3, norm_w.reshape(1, GDN_HEAD_DIM))


def _top_k_rows(s, k, payload=None):
    r = s.shape[0]
    rowi = lax.broadcasted_iota(I32, s.shape, 0)
    vals, picks = [], []
    for _ in range(k):
        mx = jnp.max(s, axis=0, keepdims=True)
        arg = jnp.min(jnp.where(s == mx, rowi, r), axis=0, keepdims=True)
        hit = rowi == arg
        vals.append(mx)
        if payload is None:
            picks.append(arg)
        else:
            picks.append(jnp.max(jnp.where(hit, payload, -1), axis=0, keepdims=True))
        s = jnp.where(hit, -jnp.inf, s)
    return jnp.concatenate(vals, axis=0), jnp.concatenate(picks, axis=0)


def _route_kernel(q_ref, keys_ref, o_ref, i1s, i2s, gts):
    tm = q_ref.shape[0]
    kk = PEER_TOPK
    i1_rows, i2_rows, gate_rows = [], [], []
    for h in range(PEER_HEADS):
        tops = []
        for p in range(2):
            c0 = (h * 2 + p) * PEER_HALF
            qhp = q_ref[:, c0:c0 + PEER_HALF].astype(BF16)
            s_t = lax.dot_general(keys_ref[h, p], qhp, NT_DIMS, preferred_element_type=F32)
            tops.append(_top_k_rows(s_t, kk))
        (s1, i1), (s2, i2) = tops
        cand_parts, id_parts = [], []
        a = 0
        while kk // (a + 1) > 1:
            nb = kk // (a + 1)
            nbp = -(-nb // SUBLANES) * SUBLANES
            part = s1[a:a + 1, :] + s2[:nbp, :]
            if nbp > nb:
                part = jnp.where(lax.broadcasted_iota(I32, part.shape, 0) < nb, part, -jnp.inf)
            cand_parts.append(part)
            id_parts.append(i1[a:a + 1, :] * PEER_NKEYS + i2[:nbp, :])
            a += 1
        cand_parts.append(s1[a:, :] + s2[0:1, :])
        id_parts.append(i1[a:, :] * PEER_NKEYS + i2[0:1, :])
        best, ids = _top_k_rows(jnp.concatenate(cand_parts, axis=0), kk,
                                payload=jnp.concatenate(id_parts, axis=0))
        e = jnp.exp(best - best[0:1, :])
        gate_rows.append(e / jnp.sum(e, axis=0, keepdims=True))
        i1_rows.append(jnp.right_shift(ids, PEER_NKEYS.bit_length() - 1))
        i2_rows.append(jnp.bitwise_and(ids, PEER_NKEYS - 1))
    i1s[...] = jnp.concatenate(i1_rows, axis=0).T
    i2s[...] = jnp.concatenate(i2_rows, axis=0).T
    gts[...] = jnp.concatenate(gate_rows, axis=0).T

    nslots = PEER_HEADS * kk
    key_iota = lax.broadcasted_iota(I32, (PEER_NKEYS, nslots), 0)

    def group(gi, carry):
        base = pl.multiple_of(gi * SUBLANES, SUBLANES)
        tiles = []
        for tk in range(SUBLANES):
            i1r = jnp.broadcast_to(i1s[pl.ds(base + tk, 1), :], (PEER_NKEYS, nslots))
            i2r = jnp.broadcast_to(i2s[pl.ds(base + tk, 1), :], (PEER_NKEYS, nslots))
            gr = jnp.broadcast_to(gts[pl.ds(base + tk, 1), :], (PEER_NKEYS, nslots))
            a_t = jnp.where(key_iota == i1r, gr, 0.0)
            a_hi = a_t.astype(BF16)
            a_lo = (a_t - a_hi.astype(F32)).astype(BF16)
            b_t = jnp.where(key_iota == i2r, 1.0, 0.0).astype(BF16)
            lhs = jnp.concatenate([a_hi, a_lo], axis=1)
            rhs = jnp.concatenate([b_t, b_t], axis=1)
            tiles.append(lax.dot_general(lhs, rhs, NT_DIMS, preferred_element_type=F32))
        o_ref[gi] = jnp.swapaxes(jnp.stack(tiles, axis=0), 0, 1)
        return carry

    lax.fori_loop(0, tm // SUBLANES, group, 0)


def _route(q, keys, tm=128):
    m, qd = q.shape
    tm = _tile(m, tm)
    nslots = PEER_HEADS * PEER_TOPK
    return pl.pallas_call(
        _route_kernel,
        grid=(m // tm,),
        in_specs=[pl.BlockSpec((tm, qd), lambda i: (i, 0)),
                  pl.BlockSpec(keys.shape, lambda i: (0, 0, 0, 0))],
        out_specs=pl.BlockSpec((tm // SUBLANES, PEER_NKEYS, SUBLANES, PEER_NKEYS), lambda i: (i, 0, 0, 0)),
        out_shape=jax.ShapeDtypeStruct((m // SUBLANES, PEER_NKEYS, SUBLANES, PEER_NKEYS), F32),
        scratch_shapes=[pltpu.VMEM((tm, nslots), I32), pltpu.VMEM((tm, nslots), I32),
                        pltpu.VMEM((tm, nslots), F32)],
        compiler_params=_cparams("parallel"),
        name="peer_route",
    )(q, keys)


def _peer_kernel(x_ref, u_ref, v_ref, g_ref, o_ref, *, i1_per_step):
    j = pl.program_id(1)
    tm = x_ref.shape[0]
    hid = lax.dot_general(x_ref[...], u_ref[...], NT_DIMS, preferred_element_type=F32)
    parts = []
    for gi in range(i1_per_step):
        gate = g_ref[:, gi].reshape(tm, PEER_NKEYS)
        parts.append((gate * _gelu(hid[:, gi * PEER_NKEYS:(gi + 1) * PEER_NKEYS])).astype(BF16))
    w = jnp.concatenate(parts, axis=1)
    contrib = jnp.dot(w, v_ref[...], preferred_element_type=F32)

    @pl.when(j == 0)
    def _():
        o_ref[...] = contrib

    @pl.when(j > 0)
    def _():
        o_ref[...] += contrib


def _peer(x1b, ub, vb, gates, tm=512, i1_per_step=4):
    m, d = x1b.shape
    ne = ub.shape[0]
    tm = _tile(m, tm)
    te = i1_per_step * PEER_NKEYS
    return pl.pallas_call(
        functools.partial(_peer_kernel, i1_per_step=i1_per_step),
        grid=(m // tm, ne // te),
        in_specs=[pl.BlockSpec((tm, d), lambda i, j: (i, 0)),
                  pl.BlockSpec((te, d), lambda i, j: (j, 0)),
                  pl.BlockSpec((te, d), lambda i, j: (j, 0)),
                  pl.BlockSpec((tm // SUBLANES, i1_per_step, SUBLANES, PEER_NKEYS), lambda i, j: (i, j, 0, 0))],
        out_specs=pl.BlockSpec((tm, d), lambda i, j: (i, 0)),
        out_shape=jax.ShapeDtypeStruct((m, d), F32),
        compiler_params=_cparams("parallel", "arbitrary"),
        name="peer_experts",
    )(x1b, ub, vb, gates)


def _layer(h, hb, p, seq, w_in, rg_conv_w, rg_conv_b, rg_wa, rg_ba, rg_wx, rg_bx, rg_lambda, rg_out,
           gdn_conv_w, gdn_a_log, gdn_dt_bias, gdn_norm_w, gdn_out, w_o, ln1_g, ln1_b, peer_wq, peer_keys,
           peer_u, peer_v, ple_w, ple_gate_w, ple_gate_b, ln2_g, ln2_b, alpha, last):
    d = h.shape[1]
    rg_w = rg_out.shape[0]
    gdn_w = gdn_out.shape[0]
    nh = gdn_w // GDN_HEAD_DIM
    n_main = 2 * rg_w + 4 * gdn_w
    w_gate_t = w_in[:, n_main:n_main + 2 * nh].T.astype(BF16)
    w_merge = w_in[:, n_main + 2 * nh:].astype(BF16)
    cols = {"rg_x": 0, "rg_y": rg_w, "q": 2 * rg_w, "k": 2 * rg_w + gdn_w, "v": 2 * rg_w + 2 * gdn_w,
            "z": 2 * rg_w + 3 * gdn_w}

    proj = _matmul(hb, w_in, tm=1024, tn=512, n=n_main, name="in_proj")
    proj_m = _matmul(hb, w_merge, tm=1024, tn=1024, name="in_proj_merge")
    gates_t = _matmul_nt(w_gate_t, hb, tm=512)

    h_a = _rglru(proj, seq, cols["rg_x"], cols["rg_y"], rg_w, rg_conv_w, rg_conv_b, rg_wa, rg_ba,
                 rg_wx, rg_bx, rg_lambda)
    o_b = _gdn(proj, gates_t, seq, cols["q"], cols["k"], cols["v"], cols["z"], gdn_w, gdn_conv_w,
               gdn_a_log, gdn_dt_bias, gdn_norm_w)
    merged = _merge(h_a, o_b, rg_out, gdn_out, proj_m, 0, d)
    mix = _matmul(merged, w_o, tm=1024, tn=512, name="out_proj")
    x1, x1b = _add_ln([h, mix], ln1_g, ln1_b, alpha, want_bf16=True)

    q = _matmul(x1b, peer_wq, tm=1024, tn=512, name="peer_query")
    gates = _route(q, peer_keys.astype(BF16))
    ffn = _peer(x1b, peer_u.astype(BF16), peer_v.astype(BF16), gates)
    ple = _ple(x1b, ple_gate_w, ple_gate_b, p, ple_w)
    out = _add_ln([x1, ffn, ple], ln2_g, ln2_b, alpha, want_bf16=not last)
    return (out[0], None) if last else tuple(out)


def kernel(x, p, ln_emb_g, ln_emb_b, w_in, rg_conv_w, rg_conv_b, rg_wa, rg_ba, rg_wx, rg_bx, rg_lambda, rg_out, gdn_conv_w, gdn_a_log, gdn_dt_bias, gdn_norm_w, gdn_out, w_o, ln1_g, ln1_b, peer_wq, peer_keys, peer_u, peer_v, ple_w, ple_gate_w, ple_gate_b, ln2_g, ln2_b):
    bsz, seq, d = x.shape
    depth = w_in.shape[0]
    m = bsz * seq
    alpha = (2.0 * depth) ** 0.25
    h, hb = _add_ln([x.reshape(m, d)], ln_emb_g, ln_emb_b, 1.0, want_bf16=True)
    for i in range(depth):
        h, hb = _layer(h, hb, p[i].reshape(m, -1), seq, w_in[i], rg_conv_w[i], rg_conv_b[i], rg_wa[i], rg_ba[i],
                       rg_wx[i], rg_bx[i], rg_lambda[i], rg_out[i], gdn_conv_w[i], gdn_a_log[i], gdn_dt_bias[i],
                       gdn_norm_w[i], gdn_out[i], w_o[i], ln1_g[i], ln1_b[i], peer_wq[i], peer_keys[i],
                       peer_u[i], peer_v[i], ple_w[i], ple_gate_w[i], ple_gate_b[i], ln2_g[i], ln2_b[i], alpha, i == depth - 1)
    return h.reshape(bsz, seq, d)
```

```python
import functools
import math

import jax
import jax.numpy as jnp
from jax import lax
from jax.experimental import pallas as pl
from jax.experimental.pallas import tpu as pltpu

F32 = jnp.float32
BF16 = jnp.bfloat16
I32 = jnp.int32

LANES = 128
SUBLANES = 8
VMEM_LIMIT = 56 * 1024 * 1024

CONV_WIDTH = 4
RG_BLOCK = 128
RG_C = 8.0
GDN_HEAD_DIM = 128
GDN_CHUNK = 64
PEER_HEADS = 8
PEER_NKEYS = 128
PEER_HALF = 128
PEER_TOPK = 16
LN_EPS = 1e-5
NORM_EPS = 1e-6
HIGHEST = lax.Precision.HIGHEST
NT_DIMS = (((1,), (1,)), ((), ()))


def _cparams(*sem):
    return pltpu.CompilerParams(dimension_semantics=sem, vmem_limit_bytes=VMEM_LIMIT)


def _tile(n, target, *also_divides):
    return math.gcd(target, n, *also_divides)


def _gelu(x):
    return 0.5 * x * (1.0 + lax.erf(x * (2.0 ** -0.5)))


def _silu(x):
    return x * jax.nn.sigmoid(x)


def _add_ln_kernel(*refs, n_in, alpha, want_bf16):
    ins = refs[:n_in]
    g_ref, b_ref = refs[n_in], refs[n_in + 1]
    outs = refs[n_in + 2:]
    x = ins[0][...]
    if alpha != 1.0:
        x = alpha * x
    for r in ins[1:]:
        x = x + r[...]
    mu = jnp.mean(x, axis=-1, keepdims=True)
    xc = x - mu
    var = jnp.mean(xc * xc, axis=-1, keepdims=True)
    y = xc * lax.rsqrt(var + LN_EPS) * g_ref[...] + b_ref[...]
    outs[0][...] = y
    if want_bf16:
        outs[1][...] = y.astype(BF16)


def _add_ln(ins, g, b, alpha, want_bf16, tm=256):
    m, d = ins[0].shape
    tm = _tile(m, tm)
    row = pl.BlockSpec((tm, d), lambda i: (i, 0))
    vec = pl.BlockSpec((1, d), lambda i: (0, 0))
    out_shape = [jax.ShapeDtypeStruct((m, d), F32)]
    if want_bf16:
        out_shape.append(jax.ShapeDtypeStruct((m, d), BF16))
    return pl.pallas_call(
        functools.partial(_add_ln_kernel, n_in=len(ins), alpha=alpha, want_bf16=want_bf16),
        grid=(m // tm,),
        in_specs=[row] * len(ins) + [vec, vec],
        out_specs=[row] * len(out_shape),
        out_shape=out_shape,
        compiler_params=_cparams("parallel"),
        name="add_ln",
    )(*ins, g.reshape(1, d), b.reshape(1, d))


def _mm_kernel(a_ref, b_ref, o_ref):
    o_ref[...] = jnp.dot(a_ref[...], b_ref[...].astype(BF16), preferred_element_type=F32).astype(o_ref.dtype)


def _matmul(a, b, tm, tn, n=None, out_dtype=F32, name="matmul"):
    m, k = a.shape
    n = b.shape[1] if n is None else n
    tm, tn = _tile(m, tm), _tile(n, tn)
    return pl.pallas_call(
        _mm_kernel,
        grid=(m // tm, n // tn),
        in_specs=[pl.BlockSpec((tm, k), lambda i, j: (i, 0)),
                  pl.BlockSpec((k, tn), lambda i, j: (0, j))],
        out_specs=pl.BlockSpec((tm, tn), lambda i, j: (i, j)),
        out_shape=jax.ShapeDtypeStruct((m, n), out_dtype),
        compiler_params=_cparams("parallel", "arbitrary"),
        name=name,
    )(a, b)


def _mm_wt_kernel(a_ref, wt_ref, o_ref):
    o_ref[...] = lax.dot_general(a_ref[...], wt_ref[...].astype(BF16), NT_DIMS, preferred_element_type=F32)


def _matmul_wt(a, wt, row0, n, tm, tn, name):
    m, k = a.shape
    tm, tn = _tile(m, tm), _tile(n, tn)
    return pl.pallas_call(
        _mm_wt_kernel,
        grid=(m // tm, n // tn),
        in_specs=[pl.BlockSpec((tm, k), lambda i, j: (i, 0)),
                  pl.BlockSpec((pl.Element(tn), pl.Element(k)),
                               lambda i, j: (pl.multiple_of(row0 + j * tn, SUBLANES), 0))],
        out_specs=pl.BlockSpec((tm, tn), lambda i, j: (i, j)),
        out_shape=jax.ShapeDtypeStruct((m, n), F32),
        compiler_params=_cparams("parallel", "arbitrary"),
        name=name,
    )(a, wt)


def _mm_nt_kernel(w_ref, a_ref, o_ref):
    o_ref[...] = lax.dot_general(w_ref[...].astype(BF16), a_ref[...], NT_DIMS, preferred_element_type=F32)


def _matmul_nt(wt, row0, r, a, tm):
    k = wt.shape[1]
    m = a.shape[0]
    tm = _tile(m, tm)
    return pl.pallas_call(
        _mm_nt_kernel,
        grid=(m // tm,),
        in_specs=[pl.BlockSpec((pl.Element(r), pl.Element(k)), lambda i: (row0, 0)),
                  pl.BlockSpec((tm, k), lambda i: (i, 0))],
        out_specs=pl.BlockSpec((r, tm), lambda i: (0, i)),
        out_shape=jax.ShapeDtypeStruct((r, m), F32),
        compiler_params=_cparams("parallel"),
        name="gate_proj",
    )(wt, a)


def _merge_kernel(ha_ref, ob_ref, wa_ref, wb_ref, ma_ref, mb_ref, o_ref):
    ya = jnp.dot(ha_ref[...], wa_ref[...].astype(BF16), preferred_element_type=F32)
    yb = jnp.dot(ob_ref[...], wb_ref[...].astype(BF16), preferred_element_type=F32)
    o_ref[...] = (jax.nn.sigmoid(ma_ref[...]) * ya + jax.nn.sigmoid(mb_ref[...]) * yb).astype(o_ref.dtype)


def _merge(ha, ob, wa, wb, proj, ma_col, mb_col, tm=1024, tn=512):
    m, k = ha.shape
    n = wa.shape[1]
    tm, tn = _tile(m, tm), _tile(n, tn, ma_col, mb_col)
    ja, jb = ma_col // tn, mb_col // tn
    return pl.pallas_call(
        _merge_kernel,
        grid=(m // tm, n // tn),
        in_specs=[pl.BlockSpec((tm, k), lambda i, j: (i, 0)),
                  pl.BlockSpec((tm, k), lambda i, j: (i, 0)),
                  pl.BlockSpec((k, tn), lambda i, j: (0, j)),
                  pl.BlockSpec((k, tn), lambda i, j: (0, j)),
                  pl.BlockSpec((tm, tn), lambda i, j: (i, ja + j)),
                  pl.BlockSpec((tm, tn), lambda i, j: (i, jb + j))],
        out_specs=pl.BlockSpec((tm, tn), lambda i, j: (i, j)),
        out_shape=jax.ShapeDtypeStruct((m, n), BF16),
        compiler_params=_cparams("parallel", "arbitrary"),
        name="merge",
    )(ha, ob, wa, wb, proj, proj)


def _ple_kernel(x_ref, wg_ref, bg_ref, p_ref, wp_ref, o_ref):
    gate = jax.nn.sigmoid(jnp.dot(x_ref[...], wg_ref[...].astype(BF16), preferred_element_type=F32) + bg_ref[...])
    o_ref[...] = gate * jnp.dot(p_ref[...].astype(BF16), wp_ref[...].astype(BF16), preferred_element_type=F32)


def _ple(x1b, wg, bg, pb, wp, tm=1024, tn=512):
    m, k = x1b.shape
    n = wg.shape[1]
    kp = pb.shape[1]
    tm, tn = _tile(m, tm), _tile(n, tn)
    return pl.pallas_call(
        _ple_kernel,
        grid=(m // tm, n // tn),
        in_specs=[pl.BlockSpec((tm, k), lambda i, j: (i, 0)),
                  pl.BlockSpec((k, tn), lambda i, j: (0, j)),
                  pl.BlockSpec((1, tn), lambda i, j: (0, j)),
                  pl.BlockSpec((tm, kp), lambda i, j: (i, 0)),
                  pl.BlockSpec((kp, tn), lambda i, j: (0, j))],
        out_specs=pl.BlockSpec((tm, tn), lambda i, j: (i, j)),
        out_shape=jax.ShapeDtypeStruct((m, n), F32),
        compiler_params=_cparams("parallel", "arbitrary"),
        name="ple",
    )(x1b, wg, bg.reshape(1, n), pb, wp)


def _causal_conv(x, w, row):
    y = x * w[CONV_WIDTH - 1:CONV_WIDTH, :]
    for j in range(CONV_WIDTH - 1):
        s = CONV_WIDTH - 1 - j
        y = y + jnp.where(row >= s, pltpu.roll(x, s, 0), 0.0) * w[j:j + 1, :]
    return y


def _rglru_kernel(x_ref, y_ref, cw_ref, cb_ref, wa_ref, ba_ref, wx_ref, bx_ref, lam_ref, o_ref):
    t, c = x_ref.shape
    row = lax.broadcasted_iota(I32, (t, c), 0)
    xr = _causal_conv(x_ref[...], cw_ref[...], row) + cb_ref[...]
    xb = xr.astype(BF16)
    rs, is_ = [], []
    for k in range(c // RG_BLOCK):
        blk = xb[:, k * RG_BLOCK:(k + 1) * RG_BLOCK]
        rs.append(jnp.dot(blk, wa_ref[k], preferred_element_type=F32))
        is_.append(jnp.dot(blk, wx_ref[k], preferred_element_type=F32))
    r = jax.nn.sigmoid(jnp.concatenate(rs, axis=1) + ba_ref[...])
    i = jax.nn.sigmoid(jnp.concatenate(is_, axis=1) + bx_ref[...])
    log_a = (-RG_C) * r * jax.nn.softplus(-lam_ref[...])
    a = jnp.exp(log_a)
    b = jnp.sqrt(-jnp.tanh(log_a) * (a * a + 1.0)) * (i * xr)
    d = 1
    while d < t:
        keep = row >= d
        a_s = jnp.where(keep, pltpu.roll(a, d, 0), 1.0)
        b_s = jnp.where(keep, pltpu.roll(b, d, 0), 0.0)
        b = a * b_s + b
        a = a * a_s
        d *= 2
    o_ref[...] = (b * _gelu(y_ref[...])).astype(o_ref.dtype)


def _rglru(proj, seq, x_col, y_col, width, cw, cb, wa, ba, wx, bx, lam, ct=256):
    m = proj.shape[0]
    nb = m // seq
    ct = _tile(width, ct, x_col, y_col)
    jx, jy = x_col // ct, y_col // ct
    kb = ct // RG_BLOCK
    vec = pl.BlockSpec((1, ct), lambda b, c: (0, c))
    gate_w = pl.BlockSpec((kb, RG_BLOCK, RG_BLOCK), lambda b, c: (c, 0, 0))
    return pl.pallas_call(
        _rglru_kernel,
        grid=(nb, width // ct),
        in_specs=[pl.BlockSpec((seq, ct), lambda b, c: (b, jx + c)),
                  pl.BlockSpec((seq, ct), lambda b, c: (b, jy + c)),
                  pl.BlockSpec((CONV_WIDTH, ct), lambda b, c: (0, c)),
                  vec, gate_w, vec, gate_w, vec, vec],
        out_specs=pl.BlockSpec((seq, ct), lambda b, c: (b, c)),
        out_shape=jax.ShapeDtypeStruct((m, width), BF16),
        compiler_params=_cparams("parallel", "parallel"),
        name="rglru",
    )(proj, proj, cw, cb.reshape(1, width), wa.astype(BF16), ba.reshape(1, width),
      wx.astype(BF16), bx.reshape(1, width), lam.reshape(1, width))


def _dot_hi(a, b):
    return jnp.dot(a, b, precision=HIGHEST, preferred_element_type=F32)


def _split_bf16(x):
    hi = x.astype(BF16)
    return hi, x - hi.astype(F32)


def _lhs3(x_dup, first):
    _, lo = _split_bf16(x_dup)
    return jnp.concatenate([jnp.where(first, x_dup, lo).astype(BF16),
                            jnp.where(first, x_dup, 0.0).astype(BF16)], axis=1)


def _rhs3(p):
    hi, lo = _split_bf16(p)
    return jnp.concatenate([hi, hi, lo.astype(BF16), jnp.zeros_like(hi)], axis=0)


def _dot3(lhs3, rhs3):
    return jnp.dot(lhs3, rhs3, preferred_element_type=F32)


def _gdn_kernel(alog_ref, dtb_ref, q_ref, k_ref, v_ref, z_ref, cwq_ref, cwk_ref, cwv_ref, gt_ref, nw_ref,
                o_ref, qs, ks, vs, us, ws, qgs, at, kdt, gcs, bts, st, *, heads_per_step, chunks_per_iter):
    t, gw = q_ref.shape
    c = GDN_CHUNK
    dh = GDN_HEAD_DIM
    nchunks = t // c
    hg = pl.program_id(1)
    nheads = gt_ref.shape[0] // 2

    row = lax.broadcasted_iota(I32, (t, gw), 0)
    q = _silu(_causal_conv(q_ref[...], cwq_ref[...], row))
    k = _silu(_causal_conv(k_ref[...], cwk_ref[...], row))
    vs[...] = _silu(_causal_conv(v_ref[...], cwv_ref[...], row))
    for g in range(heads_per_step):
        sl = slice(g * dh, (g + 1) * dh)
        qh, kh = q[:, sl], k[:, sl]
        qs[:, sl] = qh * lax.rsqrt(jnp.sum(qh * qh, axis=-1, keepdims=True) + NORM_EPS) * (dh ** -0.5)
        ks[:, sl] = kh * lax.rsqrt(jnp.sum(kh * kh, axis=-1, keepdims=True) + NORM_EPS)

    ri = lax.broadcasted_iota(I32, (c, 2 * c), 0)
    lane = lax.broadcasted_iota(I32, (c, 2 * c), 1)
    ci = jnp.bitwise_and(lane, c - 1)
    first = lane < c
    first2 = lax.broadcasted_iota(I32, (2 * c, 2 * c), 1) < c
    eye_first = ri == lane
    causal = ri >= ci
    strict = ri > ci
    eye_f = (ri == ci).astype(F32)
    upper_f = (ri <= ci).astype(F32)

    for g in range(heads_per_step):
        h = hg * heads_per_step + g
        a_gate = gt_ref[nheads + h]
        b_gate = gt_ref[h]
        neg_rate = -jnp.exp(jnp.zeros_like(a_gate) + alog_ref[h])
        g_log = neg_rate * jax.nn.softplus(a_gate + dtb_ref[h])
        gcs[g] = _dot_hi(g_log, upper_f)
        bts[g] = _dot_hi(jax.nn.sigmoid(b_gate), eye_f)
        st[g] = jnp.zeros((dh, dh), F32)

    def to_col(row_vec):
        return jnp.sum(jnp.where(eye_first, jnp.broadcast_to(row_vec, (c, 2 * c)), 0.0), axis=1, keepdims=True)

    def prepare_step(i, carry):
        chains = [(i * chunks_per_iter + j, g) for j in range(chunks_per_iter) for g in range(heads_per_step)]
        rows = [pl.ds(pl.multiple_of(n * c, c), c) for n, _ in chains]
        cols = [slice(g * dh, (g + 1) * dh) for _, g in chains]
        every = range(len(chains))
        gc_row = [gcs[g, pl.ds(n, 1), :] for n, g in chains]
        gc_col = [to_col(r) for r in gc_row]
        beta_col = [to_col(bts[g, pl.ds(n, 1), :]) for n, g in chains]
        decay = [jnp.where(causal, jnp.exp(jnp.where(causal, gc_col[x] - gc_row[x], 0.0)), 0.0) for x in every]
        e_col = [jnp.exp(cv) for cv in gc_col]
        qc = [qs[rows[x], cols[x]] for x in every]
        kc = [ks[rows[x], cols[x]] for x in every]
        vc = [vs[rows[x], cols[x]] for x in every]
        kb = [kc[x] * beta_col[x] for x in every]

        kk = []
        k2_hi = []
        for x in every:
            kb_hi, kb_lo = _split_bf16(kb[x])
            hi, lo = _split_bf16(jnp.concatenate([kc[x], kc[x]], axis=0))
            k2_hi.append(hi)
            kk.append(lax.dot_general(jnp.concatenate([kb_hi, kb_lo.astype(BF16), kb_hi], axis=1),
                                      jnp.concatenate([hi, hi, lo.astype(BF16)], axis=1),
                                      NT_DIMS, preferred_element_type=F32))
        a_low = [jnp.where(strict, kk[x] * decay[x], 0.0) for x in every]
        inv = [eye_f - a for a in a_low]
        pw = [_dot3(_lhs3(a, first), _rhs3(a)) for a in a_low]
        span = 2
        while 2 * span < c:
            both = [_dot3(_lhs3(jnp.concatenate([inv[x], pw[x]], axis=0), first2), _rhs3(pw[x])) for x in every]
            inv = [inv[x] + both[x][:c] for x in every]
            pw = [b[c:] for b in both]
            span *= 2
        inv = [inv[x] + _dot3(_lhs3(inv[x], first), _rhs3(pw[x])) for x in every]
        sol = [_dot3(_lhs3(inv[x], first),
                     _rhs3(jnp.concatenate([vc[x] * beta_col[x], kb[x] * e_col[x]], axis=1))) for x in every]
        attn = [lax.dot_general(qc[x].astype(BF16), k2_hi[x], NT_DIMS, preferred_element_type=F32) * decay[x]
                for x in every]
        for x, (n, g) in enumerate(chains):
            us[rows[x], cols[x]] = sol[x][:, :dh]
            ws[rows[x], cols[x]] = sol[x][:, dh:].astype(BF16)
            at[g, rows[x], :] = attn[x].astype(BF16)
            qgs[rows[x], cols[x]] = (qc[x] * e_col[x]).astype(BF16)
            kdt[g, n] = (kc[x] * jnp.exp(gc_row[x][:, c - 1:c] - gc_col[x])).T.astype(BF16)
        return carry

    lax.fori_loop(0, nchunks // chunks_per_iter, prepare_step, 0)

    def state_step(n, carry):
        rows = pl.ds(pl.multiple_of(n * c, c), c)
        heads = range(heads_per_step)
        cols = [slice(g * dh, (g + 1) * dh) for g in heads]
        state = [st[g] for g in heads]
        state_b = [s.astype(BF16) for s in state]
        decay_last = [jnp.exp(gcs[g, pl.ds(n, 1), :][:, c - 1:c]) for g in heads]
        v_new = [us[rows, cols[g]] - jnp.dot(ws[rows, cols[g]], state_b[g], preferred_element_type=F32)
                 for g in heads]
        v_new_b = [v.astype(BF16) for v in v_new]
        o_state = [jnp.dot(qgs[rows, cols[g]], state_b[g], preferred_element_type=F32) for g in heads]
        o_c = [o_state[g] + jnp.dot(at[g, rows, :][:, :c], v_new_b[g], preferred_element_type=F32) for g in heads]
        for g in heads:
            st[g] = state[g] * decay_last[g] + jnp.dot(kdt[g, n], v_new_b[g], preferred_element_type=F32)
        for g in heads:
            o_n = o_c[g] * lax.rsqrt(jnp.mean(o_c[g] * o_c[g], axis=-1, keepdims=True) + NORM_EPS)
            o_ref[rows, cols[g]] = (o_n * nw_ref[...] * _silu(z_ref[rows, cols[g]])).astype(o_ref.dtype)
        return carry

    lax.fori_loop(0, nchunks, state_step, 0)


def _gdn(proj, gates_t, seq, q_col, k_col, v_col, z_col, width, conv_w, a_log, dt_bias, norm_w,
         heads_per_step=2, chunks_per_iter=4):
    m = proj.shape[0]
    nb = m // seq
    gw = heads_per_step * GDN_HEAD_DIM
    nh = width // GDN_HEAD_DIM
    nchunks = seq // GDN_CHUNK
    jq, jk, jv, jz = (col // gw for col in (q_col, k_col, v_col, z_col))
    wblocks = width // gw
    gt3 = gates_t.reshape(2 * nh, m // GDN_CHUNK, GDN_CHUNK)
    smem = pl.BlockSpec(memory_space=pltpu.SMEM)

    def col_spec(j0):
        return pl.BlockSpec((seq, gw), lambda b, h: (b, j0 + h))

    def cw_spec(j0):
        return pl.BlockSpec((CONV_WIDTH, gw), lambda b, h: (0, j0 + h))

    return pl.pallas_call(
        functools.partial(_gdn_kernel, heads_per_step=heads_per_step, chunks_per_iter=chunks_per_iter),
        grid=(nb, nh // heads_per_step),
        in_specs=[smem, smem, col_spec(jq), col_spec(jk), col_spec(jv), col_spec(jz),
                  cw_spec(0), cw_spec(wblocks), cw_spec(2 * wblocks),
                  pl.BlockSpec((2 * nh, nchunks, GDN_CHUNK), lambda b, h: (0, b, 0)),
                  pl.BlockSpec((1, GDN_HEAD_DIM), lambda b, h: (0, 0))],
        out_specs=pl.BlockSpec((seq, gw), lambda b, h: (b, h)),
        out_shape=jax.ShapeDtypeStruct((m, width), BF16),
        scratch_shapes=[pltpu.VMEM((seq, gw), F32), pltpu.VMEM((seq, gw), F32), pltpu.VMEM((seq, gw), F32),
                        pltpu.VMEM((seq, gw), F32), pltpu.VMEM((seq, gw), BF16), pltpu.VMEM((seq, gw), BF16),
                        pltpu.VMEM((heads_per_step, seq, 2 * GDN_CHUNK), BF16),
                        pltpu.VMEM((heads_per_step, nchunks, GDN_HEAD_DIM, GDN_CHUNK), BF16),
                        pltpu.VMEM((heads_per_step, nchunks, 2 * GDN_CHUNK), F32),
                        pltpu.VMEM((heads_per_step, nchunks, 2 * GDN_CHUNK), F32),
                        pltpu.VMEM((heads_per_step, GDN_HEAD_DIM, GDN_HEAD_DIM), F32)],
        compiler_params=_cparams("parallel", "parallel"),
        name="gdn",
    )(a_log, dt_bias, proj, proj, proj, proj, conv_w, conv_w, conv_w, gt3, norm_w.reshape(1, GDN_HEAD_DIM))


def _top_k_rows(s, k, payload=None):
    r = s.shape[0]
    rowi = lax.broadcasted_iota(I32, s.shape, 0)
    vals, picks = [], []
    for _ in range(k):
        mx = jnp.max(s, axis=0, keepdims=True)
        arg = jnp.min(jnp.where(s == mx, rowi, r), axis=0, keepdims=True)
        hit = rowi == arg
        vals.append(mx)
        if payload is None:
            picks.append(arg)
        else:
            picks.append(jnp.max(jnp.where(hit, payload, -1), axis=0, keepdims=True))
        s = jnp.where(hit, -jnp.inf, s)
    return jnp.concatenate(vals, axis=0), jnp.concatenate(picks, axis=0)


def _route_kernel(q_ref, keys_ref, o_ref, i1s, i2s, gts):
    tm = q_ref.shape[0]
    kk = PEER_TOPK
    i1_rows, i2_rows, gate_rows = [], [], []
    for h in range(PEER_HEADS):
        tops = []
        for p in range(2):
            c0 = (h * 2 + p) * PEER_HALF
            qhp = q_ref[:, c0:c0 + PEER_HALF].astype(BF16)
            s_t = lax.dot_general(keys_ref[h, p], qhp, NT_DIMS, preferred_element_type=F32)
            tops.append(_top_k_rows(s_t, kk))
        (s1, i1), (s2, i2) = tops
        cand_parts, id_parts = [], []
        a = 0
        while kk // (a + 1) > 1:
            nb = kk // (a + 1)
            nbp = -(-nb // SUBLANES) * SUBLANES
            part = s1[a:a + 1, :] + s2[:nbp, :]
            if nbp > nb:
                part = jnp.where(lax.broadcasted_iota(I32, part.shape, 0) < nb, part, -jnp.inf)
            cand_parts.append(part)
            id_parts.append(i1[a:a + 1, :] * PEER_NKEYS + i2[:nbp, :])
            a += 1
        cand_parts.append(s1[a:, :] + s2[0:1, :])
        id_parts.append(i1[a:, :] * PEER_NKEYS + i2[0:1, :])
        best, ids = _top_k_rows(jnp.concatenate(cand_parts, axis=0), kk,
                                payload=jnp.concatenate(id_parts, axis=0))
        e = jnp.exp(best - best[0:1, :])
        gate_rows.append(e / jnp.sum(e, axis=0, keepdims=True))
        i1_rows.append(jnp.right_shift(ids, PEER_NKEYS.bit_length() - 1))
        i2_rows.append(jnp.bitwise_and(ids, PEER_NKEYS - 1))
    i1s[...] = jnp.concatenate(i1_rows, axis=0).T
    i2s[...] = jnp.concatenate(i2_rows, axis=0).T
    gts[...] = jnp.concatenate(gate_rows, axis=0).T

    nslots = PEER_HEADS * kk
    key_iota = lax.broadcasted_iota(I32, (PEER_NKEYS, nslots), 0)

    def group(gi, carry):
        base = pl.multiple_of(gi * SUBLANES, SUBLANES)
        tiles = []
        for tk in range(SUBLANES):
            i1r = jnp.broadcast_to(i1s[pl.ds(base + tk, 1), :], (PEER_NKEYS, nslots))
            i2r = jnp.broadcast_to(i2s[pl.ds(base + tk, 1), :], (PEER_NKEYS, nslots))
            gr = jnp.broadcast_to(gts[pl.ds(base + tk, 1), :], (PEER_NKEYS, nslots))
            a_t = jnp.where(key_iota == i1r, gr, 0.0)
            a_hi = a_t.astype(BF16)
            a_lo = (a_t - a_hi.astype(F32)).astype(BF16)
            b_t = jnp.where(key_iota == i2r, 1.0, 0.0).astype(BF16)
            lhs = jnp.concatenate([a_hi, a_lo], axis=1)
            rhs = jnp.concatenate([b_t, b_t], axis=1)
            tiles.append(lax.dot_general(lhs, rhs, NT_DIMS, preferred_element_type=F32))
        o_ref[gi] = jnp.swapaxes(jnp.stack(tiles, axis=0), 0, 1)
        return carry

    lax.fori_loop(0, tm // SUBLANES, group, 0)


def _route(q, keys, tm=128):
    m, qd = q.shape
    tm = _tile(m, tm)
    nslots = PEER_HEADS * PEER_TOPK
    return pl.pallas_call(
        _route_kernel,
        grid=(m // tm,),
        in_specs=[pl.BlockSpec((tm, qd), lambda i: (i, 0)),
                  pl.BlockSpec(keys.shape, lambda i: (0, 0, 0, 0))],
        out_specs=pl.BlockSpec((tm // SUBLANES, PEER_NKEYS, SUBLANES, PEER_NKEYS), lambda i: (i, 0, 0, 0)),
        out_shape=jax.ShapeDtypeStruct((m // SUBLANES, PEER_NKEYS, SUBLANES, PEER_NKEYS), F32),
        scratch_shapes=[pltpu.VMEM((tm, nslots), I32), pltpu.VMEM((tm, nslots), I32),
                        pltpu.VMEM((tm, nslots), F32)],
        compiler_params=_cparams("parallel"),
        name="peer_route",
    )(q, keys)


def _peer_kernel(x_ref, u_ref, v_ref, g_ref, o_ref, w_scr, *, i1_per_step):
    j = pl.program_id(1)
    last = pl.num_programs(1) - 1
    tm = x_ref.shape[0]

    def hidden():
        hid = lax.dot_general(x_ref[...], u_ref[...], NT_DIMS, preferred_element_type=F32)
        parts = []
        for gi in range(i1_per_step):
            gate = g_ref[:, gi].reshape(tm, PEER_NKEYS)
            parts.append((gate * _gelu(hid[:, gi * PEER_NKEYS:(gi + 1) * PEER_NKEYS])).astype(BF16))
        return jnp.concatenate(parts, axis=1)

    @pl.when(j == 0)
    def _():
        o_ref[...] = jnp.zeros_like(o_ref)
        w_scr[...] = hidden()

    @pl.when(jnp.logical_and(j > 0, j < last))
    def _():
        w_prev = w_scr[...]
        w_scr[...] = hidden()
        o_ref[...] += jnp.dot(w_prev, v_ref[...], preferred_element_type=F32)

    @pl.when(j == last)
    def _():
        o_ref[...] += jnp.dot(w_scr[...], v_ref[...], preferred_element_type=F32)


def _peer(x1b, ub, vb, gates, tm=512, i1_per_step=4):
    m, d = x1b.shape
    ne = ub.shape[0]
    tm = _tile(m, tm)
    te = i1_per_step * PEER_NKEYS
    nblk = ne // te
    return pl.pallas_call(
        functools.partial(_peer_kernel, i1_per_step=i1_per_step),
        grid=(m // tm, nblk + 1),
        in_specs=[pl.BlockSpec((tm, d), lambda i, j: (i, 0)),
                  pl.BlockSpec((te, d), lambda i, j: (jnp.minimum(j, nblk - 1), 0)),
                  pl.BlockSpec((te, d), lambda i, j: (jnp.maximum(j - 1, 0), 0)),
                  pl.BlockSpec((tm // SUBLANES, i1_per_step, SUBLANES, PEER_NKEYS),
                               lambda i, j: (i, jnp.minimum(j, nblk - 1), 0, 0))],
        out_specs=pl.BlockSpec((tm, d), lambda i, j: (i, 0)),
        out_shape=jax.ShapeDtypeStruct((m, d), F32),
        scratch_shapes=[pltpu.VMEM((tm, te), BF16)],
        compiler_params=_cparams("parallel", "arbitrary"),
        name="peer_experts",
    )(x1b, ub, vb, gates)


def _layer(h, hb, p, seq, w_in, rg_conv_w, rg_conv_b, rg_wa, rg_ba, rg_wx, rg_bx, rg_lambda, rg_out,
           gdn_conv_w, gdn_a_log, gdn_dt_bias, gdn_norm_w, gdn_out, w_o, ln1_g, ln1_b, peer_wq, peer_keys,
           peer_u, peer_v, ple_w, ple_gate_w, ple_gate_b, ln2_g, ln2_b, alpha, last):
    d = h.shape[1]
    rg_w = rg_out.shape[0]
    gdn_w = gdn_out.shape[0]
    nh = gdn_w // GDN_HEAD_DIM
    n_main = 2 * rg_w + 4 * gdn_w
    w_in_t = w_in.T
    cols = {"rg_x": 0, "rg_y": rg_w, "q": 2 * rg_w, "k": 2 * rg_w + gdn_w, "v": 2 * rg_w + 2 * gdn_w,
            "z": 2 * rg_w + 3 * gdn_w}

    proj = _matmul_wt(hb, w_in_t, 0, n_main, tm=1024, tn=512, name="in_proj")
    proj_m = _matmul_wt(hb, w_in_t, n_main + 2 * nh, 2 * d, tm=1024, tn=512, name="in_proj_merge")
    gates_t = _matmul_nt(w_in_t, n_main, 2 * nh, hb, tm=512)

    h_a = _rglru(proj, seq, cols["rg_x"], cols["rg_y"], rg_w, rg_conv_w, rg_conv_b, rg_wa, rg_ba,
                 rg_wx, rg_bx, rg_lambda)
    o_b = _gdn(proj, gates_t, seq, cols["q"], cols["k"], cols["v"], cols["z"], gdn_w, gdn_conv_w,
               gdn_a_log, gdn_dt_bias, gdn_norm_w)
    merged = _merge(h_a, o_b, rg_out, gdn_out, proj_m, 0, d)
    mix = _matmul(merged, w_o, tm=1024, tn=512, name="out_proj")
    x1, x1b = _add_ln([h, mix], ln1_g, ln1_b, alpha, want_bf16=True)

    q = _matmul(x1b, peer_wq, tm=1024, tn=512, name="peer_query")
    gates = _route(q, peer_keys.astype(BF16))
    ffn = _peer(x1b, peer_u.astype(BF16), peer_v.astype(BF16), gates)
    ple = _ple(x1b, ple_gate_w, ple_gate_b, p, ple_w)
    out = _add_ln([x1, ffn, ple], ln2_g, ln2_b, alpha, want_bf16=not last)
    return (out[0], None) if last else tuple(out)


def kernel(x, p, ln_emb_g, ln_emb_b, w_in, rg_conv_w, rg_conv_b, rg_wa, rg_ba, rg_wx, rg_bx, rg_lambda, rg_out, gdn_conv_w, gdn_a_log, gdn_dt_bias, gdn_norm_w, gdn_out, w_o, ln1_g, ln1_b, peer_wq, peer_keys, peer_u, peer_v, ple_w, ple_gate_w, ple_gate_b, ln2_g, ln2_b):
    bsz, seq, d = x.shape
    depth = w_in.shape[0]
    m = bsz * seq
    alpha = (2.0 * depth) ** 0.25
    h, hb = _add_ln([x.reshape(m, d)], ln_emb_g, ln_emb_b, 1.0, want_bf16=True)
    for i in range(depth):
        h, hb = _layer(h, hb, p[i].reshape(m, -1), seq, w_in[i], rg_conv_w[i], rg_conv_b[i], rg_wa[i], rg_ba[i],
                       rg_wx[i], rg_bx[i], rg_lambda[i], rg_out[i], gdn_conv_w[i], gdn_a_log[i], gdn_dt_bias[i],
                       gdn_norm_w[i], gdn_out[i], w_o[i], ln1_g[i], ln1_b[i], peer_wq[i], peer_keys[i],
                       peer_u[i], peer_v[i], ple_w[i], ple_gate_w[i], ple_gate_b[i], ln2_g[i], ln2_b[i], alpha, i == depth - 1)
    return h.reshape(bsz, seq, d)
```

```python
import functools
import math

import jax
import jax.numpy as jnp
from jax import lax
from jax.experimental import pallas as pl
from jax.experimental.pallas import tpu as pltpu

F32 = jnp.float32
BF16 = jnp.bfloat16
I32 = jnp.int32

LANES = 128
SUBLANES = 8
VMEM_LIMIT = 56 * 1024 * 1024

CONV_WIDTH = 4
RG_BLOCK = 128
RG_C = 8.0
GDN_HEAD_DIM = 128
GDN_CHUNK = 64
PEER_HEADS = 8
PEER_NKEYS = 128
PEER_HALF = 128
PEER_TOPK = 16
ROUTE_GROUPS_PER_ITER = 4
LN_EPS = 1e-5
NORM_EPS = 1e-6
HIGHEST = lax.Precision.HIGHEST
NT_DIMS = (((1,), (1,)), ((), ()))


def _cparams(*sem):
    return pltpu.CompilerParams(dimension_semantics=sem, vmem_limit_bytes=VMEM_LIMIT)


def _tile(n, target, *also_divides):
    return math.gcd(target, n, *also_divides)


def _gelu(x):
    return 0.5 * x * (1.0 + lax.erf(x * (2.0 ** -0.5)))


def _silu(x):
    return x * jax.nn.sigmoid(x)


def _add_ln_kernel(*refs, n_in, alpha, want_bf16):
    ins = refs[:n_in]
    g_ref, b_ref = refs[n_in], refs[n_in + 1]
    outs = refs[n_in + 2:]
    x = ins[0][...]
    if alpha != 1.0:
        x = alpha * x
    for r in ins[1:]:
        x = x + r[...]
    mu = jnp.mean(x, axis=-1, keepdims=True)
    xc = x - mu
    var = jnp.mean(xc * xc, axis=-1, keepdims=True)
    y = xc * lax.rsqrt(var + LN_EPS) * g_ref[...] + b_ref[...]
    outs[0][...] = y
    if want_bf16:
        outs[1][...] = y.astype(BF16)


def _add_ln(ins, g, b, alpha, want_bf16, tm=256):
    m, d = ins[0].shape
    tm = _tile(m, tm)
    row = pl.BlockSpec((tm, d), lambda i: (i, 0))
    vec = pl.BlockSpec((1, d), lambda i: (0, 0))
    out_shape = [jax.ShapeDtypeStruct((m, d), F32)]
    if want_bf16:
        out_shape.append(jax.ShapeDtypeStruct((m, d), BF16))
    return pl.pallas_call(
        functools.partial(_add_ln_kernel, n_in=len(ins), alpha=alpha, want_bf16=want_bf16),
        grid=(m // tm,),
        in_specs=[row] * len(ins) + [vec, vec],
        out_specs=[row] * len(out_shape),
        out_shape=out_shape,
        compiler_params=_cparams("parallel"),
        name="add_ln",
    )(*ins, g.reshape(1, d), b.reshape(1, d))


def _mm_kernel(a_ref, b_ref, o_ref):
    o_ref[...] = jnp.dot(a_ref[...], b_ref[...].astype(BF16), preferred_element_type=F32).astype(o_ref.dtype)


def _matmul(a, b, tm, tn, n=None, out_dtype=F32, name="matmul"):
    m, k = a.shape
    n = b.shape[1] if n is None else n
    tm, tn = _tile(m, tm), _tile(n, tn)
    return pl.pallas_call(
        _mm_kernel,
        grid=(m // tm, n // tn),
        in_specs=[pl.BlockSpec((tm, k), lambda i, j: (i, 0)),
                  pl.BlockSpec((k, tn), lambda i, j: (0, j))],
        out_specs=pl.BlockSpec((tm, tn), lambda i, j: (i, j)),
        out_shape=jax.ShapeDtypeStruct((m, n), out_dtype),
        compiler_params=_cparams("parallel", "arbitrary"),
        name=name,
    )(a, b)


def _mm_wt_kernel(a_ref, wt_ref, o_ref):
    o_ref[...] = lax.dot_general(a_ref[...], wt_ref[...].astype(BF16), NT_DIMS, preferred_element_type=F32)


def _matmul_wt(a, wt, row0, n, tm, tn, name):
    m, k = a.shape
    tm, tn = _tile(m, tm), _tile(n, tn)
    return pl.pallas_call(
        _mm_wt_kernel,
        grid=(m // tm, n // tn),
        in_specs=[pl.BlockSpec((tm, k), lambda i, j: (i, 0)),
                  pl.BlockSpec((pl.Element(tn), pl.Element(k)),
                               lambda i, j: (pl.multiple_of(row0 + j * tn, SUBLANES), 0))],
        out_specs=pl.BlockSpec((tm, tn), lambda i, j: (i, j)),
        out_shape=jax.ShapeDtypeStruct((m, n), F32),
        compiler_params=_cparams("parallel", "arbitrary"),
        name=name,
    )(a, wt)


def _mm_nt_kernel(w_ref, a_ref, o_ref):
    o_ref[...] = lax.dot_general(w_ref[...].astype(BF16), a_ref[...], NT_DIMS, preferred_element_type=F32)


def _matmul_nt(wt, row0, r, a, tm):
    k = wt.shape[1]
    m = a.shape[0]
    tm = _tile(m, tm)
    return pl.pallas_call(
        _mm_nt_kernel,
        grid=(m // tm,),
        in_specs=[pl.BlockSpec((pl.Element(r), pl.Element(k)), lambda i: (row0, 0)),
                  pl.BlockSpec((tm, k), lambda i: (i, 0))],
        out_specs=pl.BlockSpec((r, tm), lambda i: (0, i)),
        out_shape=jax.ShapeDtypeStruct((r, m), F32),
        compiler_params=_cparams("parallel"),
        name="gate_proj",
    )(wt, a)


def _merge_kernel(ha_ref, ob_ref, wa_ref, wb_ref, ma_ref, mb_ref, o_ref):
    ya = jnp.dot(ha_ref[...], wa_ref[...].astype(BF16), preferred_element_type=F32)
    yb = jnp.dot(ob_ref[...], wb_ref[...].astype(BF16), preferred_element_type=F32)
    o_ref[...] = (jax.nn.sigmoid(ma_ref[...]) * ya + jax.nn.sigmoid(mb_ref[...]) * yb).astype(o_ref.dtype)


def _merge(ha, ob, wa, wb, proj, ma_col, mb_col, tm=1024, tn=512):
    m, k = ha.shape
    n = wa.shape[1]
    tm, tn = _tile(m, tm), _tile(n, tn, ma_col, mb_col)
    ja, jb = ma_col // tn, mb_col // tn
    return pl.pallas_call(
        _merge_kernel,
        grid=(m // tm, n // tn),
        in_specs=[pl.BlockSpec((tm, k), lambda i, j: (i, 0)),
                  pl.BlockSpec((tm, k), lambda i, j: (i, 0)),
                  pl.BlockSpec((k, tn), lambda i, j: (0, j)),
                  pl.BlockSpec((k, tn), lambda i, j: (0, j)),
                  pl.BlockSpec((tm, tn), lambda i, j: (i, ja + j)),
                  pl.BlockSpec((tm, tn), lambda i, j: (i, jb + j))],
        out_specs=pl.BlockSpec((tm, tn), lambda i, j: (i, j)),
        out_shape=jax.ShapeDtypeStruct((m, n), BF16),
        compiler_params=_cparams("parallel", "arbitrary"),
        name="merge",
    )(ha, ob, wa, wb, proj, proj)


def _ple_kernel(x_ref, wg_ref, bg_ref, p_ref, wp_ref, o_ref):
    gate = jax.nn.sigmoid(jnp.dot(x_ref[...], wg_ref[...].astype(BF16), preferred_element_type=F32) + bg_ref[...])
    o_ref[...] = gate * jnp.dot(p_ref[...].astype(BF16), wp_ref[...].astype(BF16), preferred_element_type=F32)


def _ple(x1b, wg, bg, pb, wp, tm=1024, tn=512):
    m, k = x1b.shape
    n = wg.shape[1]
    kp = pb.shape[1]
    tm, tn = _tile(m, tm), _tile(n, tn)
    return pl.pallas_call(
        _ple_kernel,
        grid=(m // tm, n // tn),
        in_specs=[pl.BlockSpec((tm, k), lambda i, j: (i, 0)),
                  pl.BlockSpec((k, tn), lambda i, j: (0, j)),
                  pl.BlockSpec((1, tn), lambda i, j: (0, j)),
                  pl.BlockSpec((tm, kp), lambda i, j: (i, 0)),
                  pl.BlockSpec((kp, tn), lambda i, j: (0, j))],
        out_specs=pl.BlockSpec((tm, tn), lambda i, j: (i, j)),
        out_shape=jax.ShapeDtypeStruct((m, n), F32),
        compiler_params=_cparams("parallel", "arbitrary"),
        name="ple",
    )(x1b, wg, bg.reshape(1, n), pb, wp)


def _causal_conv(x, w, row):
    y = x * w[CONV_WIDTH - 1:CONV_WIDTH, :]
    for j in range(CONV_WIDTH - 1):
        s = CONV_WIDTH - 1 - j
        y = y + jnp.where(row >= s, pltpu.roll(x, s, 0), 0.0) * w[j:j + 1, :]
    return y


def _rglru_kernel(x_ref, y_ref, cw_ref, cb_ref, wa_ref, ba_ref, wx_ref, bx_ref, lam_ref, o_ref):
    t, c = x_ref.shape
    row = lax.broadcasted_iota(I32, (t, c), 0)
    xr = _causal_conv(x_ref[...], cw_ref[...], row) + cb_ref[...]
    xb = xr.astype(BF16)
    rs, is_ = [], []
    for k in range(c // RG_BLOCK):
        blk = xb[:, k * RG_BLOCK:(k + 1) * RG_BLOCK]
        rs.append(jnp.dot(blk, wa_ref[k], preferred_element_type=F32))
        is_.append(jnp.dot(blk, wx_ref[k], preferred_element_type=F32))
    r = jax.nn.sigmoid(jnp.concatenate(rs, axis=1) + ba_ref[...])
    i = jax.nn.sigmoid(jnp.concatenate(is_, axis=1) + bx_ref[...])
    log_a = (-RG_C) * r * jax.nn.softplus(-lam_ref[...])
    a = jnp.exp(log_a)
    b = jnp.sqrt(-jnp.tanh(log_a) * (a * a + 1.0)) * (i * xr)
    d = 1
    while d < t:
        keep = row >= d
        a_s = jnp.where(keep, pltpu.roll(a, d, 0), 1.0)
        b_s = jnp.where(keep, pltpu.roll(b, d, 0), 0.0)
        b = a * b_s + b
        a = a * a_s
        d *= 2
    o_ref[...] = (b * _gelu(y_ref[...])).astype(o_ref.dtype)


def _rglru(proj, seq, x_col, y_col, width, cw, cb, wa, ba, wx, bx, lam, ct=256):
    m = proj.shape[0]
    nb = m // seq
    ct = _tile(width, ct, x_col, y_col)
    jx, jy = x_col // ct, y_col // ct
    kb = ct // RG_BLOCK
    vec = pl.BlockSpec((1, ct), lambda b, c: (0, c))
    gate_w = pl.BlockSpec((kb, RG_BLOCK, RG_BLOCK), lambda b, c: (c, 0, 0))
    return pl.pallas_call(
        _rglru_kernel,
        grid=(nb, width // ct),
        in_specs=[pl.BlockSpec((seq, ct), lambda b, c: (b, jx + c)),
                  pl.BlockSpec((seq, ct), lambda b, c: (b, jy + c)),
                  pl.BlockSpec((CONV_WIDTH, ct), lambda b, c: (0, c)),
                  vec, gate_w, vec, gate_w, vec, vec],
        out_specs=pl.BlockSpec((seq, ct), lambda b, c: (b, c)),
        out_shape=jax.ShapeDtypeStruct((m, width), BF16),
        compiler_params=_cparams("parallel", "parallel"),
        name="rglru",
    )(proj, proj, cw, cb.reshape(1, width), wa.astype(BF16), ba.reshape(1, width),
      wx.astype(BF16), bx.reshape(1, width), lam.reshape(1, width))


def _dot_hi(a, b):
    return jnp.dot(a, b, precision=HIGHEST, preferred_element_type=F32)


def _split_bf16(x):
    hi = x.astype(BF16)
    return hi, x - hi.astype(F32)


def _lhs3(x_dup, first):
    _, lo = _split_bf16(x_dup)
    return jnp.concatenate([jnp.where(first, x_dup, lo).astype(BF16),
                            jnp.where(first, x_dup, 0.0).astype(BF16)], axis=1)


def _rhs3(p):
    hi, lo = _split_bf16(p)
    return jnp.concatenate([hi, hi, lo.astype(BF16), jnp.zeros_like(hi)], axis=0)


def _dot3(lhs3, rhs3):
    return jnp.dot(lhs3, rhs3, preferred_element_type=F32)


def _gdn_kernel(alog_ref, dtb_ref, q_ref, k_ref, v_ref, z_ref, cwq_ref, cwk_ref, cwv_ref, gt_ref, nw_ref,
                o_ref, qs, ks, vs, aus, qws, kws, kus, gcs, bts, *, heads_per_step, chunks_per_iter):
    t, gw = q_ref.shape
    c = GDN_CHUNK
    dh = GDN_HEAD_DIM
    nchunks = t // c
    hg = pl.program_id(1)
    nheads = gt_ref.shape[0] // 2

    row = lax.broadcasted_iota(I32, (t, gw), 0)
    q = _silu(_causal_conv(q_ref[...], cwq_ref[...], row))
    k = _silu(_causal_conv(k_ref[...], cwk_ref[...], row))
    vs[...] = _silu(_causal_conv(v_ref[...], cwv_ref[...], row))
    for g in range(heads_per_step):
        sl = slice(g * dh, (g + 1) * dh)
        qh, kh = q[:, sl], k[:, sl]
        qs[:, sl] = qh * lax.rsqrt(jnp.sum(qh * qh, axis=-1, keepdims=True) + NORM_EPS) * (dh ** -0.5)
        ks[:, sl] = kh * lax.rsqrt(jnp.sum(kh * kh, axis=-1, keepdims=True) + NORM_EPS)

    ri = lax.broadcasted_iota(I32, (c, 2 * c), 0)
    lane = lax.broadcasted_iota(I32, (c, 2 * c), 1)
    ci = jnp.bitwise_and(lane, c - 1)
    first = lane < c
    first2 = lax.broadcasted_iota(I32, (2 * c, 2 * c), 1) < c
    eye_first = ri == lane
    causal = ri >= ci
    strict = ri > ci
    eye_f = (ri == ci).astype(F32)
    upper_f = (ri <= ci).astype(F32)

    for g in range(heads_per_step):
        h = hg * heads_per_step + g
        a_gate = gt_ref[nheads + h]
        b_gate = gt_ref[h]
        neg_rate = -jnp.exp(jnp.zeros_like(a_gate) + alog_ref[h])
        g_log = neg_rate * jax.nn.softplus(a_gate + dtb_ref[h])
        gcs[g] = _dot_hi(g_log, upper_f)
        bts[g] = _dot_hi(jax.nn.sigmoid(b_gate), eye_f)

    def to_col(row_vec):
        return jnp.sum(jnp.where(eye_first, jnp.broadcast_to(row_vec, (c, 2 * c)), 0.0), axis=1, keepdims=True)

    def prepare_step(i, carry):
        chains = [(i * chunks_per_iter + j, g) for j in range(chunks_per_iter) for g in range(heads_per_step)]
        rows = [pl.ds(pl.multiple_of(n * c, c), c) for n, _ in chains]
        cols = [slice(g * dh, (g + 1) * dh) for _, g in chains]
        every = range(len(chains))
        gc_row = [gcs[g, pl.ds(n, 1), :] for n, g in chains]
        gc_col = [to_col(r) for r in gc_row]
        beta_col = [to_col(bts[g, pl.ds(n, 1), :]) for n, g in chains]
        decay = [jnp.where(causal, jnp.exp(jnp.where(causal, gc_col[x] - gc_row[x], 0.0)), 0.0) for x in every]
        e_col = [jnp.exp(cv) for cv in gc_col]
        qc = [qs[rows[x], cols[x]] for x in every]
        kc = [ks[rows[x], cols[x]] for x in every]
        vc = [vs[rows[x], cols[x]] for x in every]
        kb = [kc[x] * beta_col[x] for x in every]

        kk = []
        k2_hi = []
        for x in every:
            kb_hi, kb_lo = _split_bf16(kb[x])
            hi, lo = _split_bf16(jnp.concatenate([kc[x], kc[x]], axis=0))
            k2_hi.append(hi)
            kk.append(lax.dot_general(jnp.concatenate([kb_hi, kb_lo.astype(BF16), kb_hi], axis=1),
                                      jnp.concatenate([hi, hi, lo.astype(BF16)], axis=1),
                                      NT_DIMS, preferred_element_type=F32))
        a_low = [jnp.where(strict, kk[x] * decay[x], 0.0) for x in every]
        inv = [eye_f - a for a in a_low]
        pw = [_dot3(_lhs3(a, first), _rhs3(a)) for a in a_low]
        span = 2
        while 2 * span < c:
            both = [_dot3(_lhs3(jnp.concatenate([inv[x], pw[x]], axis=0), first2), _rhs3(pw[x])) for x in every]
            inv = [inv[x] + both[x][:c] for x in every]
            pw = [b[c:] for b in both]
            span *= 2
        inv = [inv[x] + _dot3(_lhs3(inv[x], first), _rhs3(pw[x])) for x in every]
        sol = [_dot3(_lhs3(inv[x], first),
                     _rhs3(jnp.concatenate([vc[x] * beta_col[x], kb[x] * e_col[x]], axis=1))) for x in every]
        attn = [(lax.dot_general(qc[x].astype(BF16), k2_hi[x], NT_DIMS, preferred_element_type=F32)
                 * decay[x])[:, :c].astype(BF16) for x in every]
        u_b = [s[:, :dh].astype(BF16) for s in sol]
        w_b = [s[:, dh:].astype(BF16) for s in sol]
        kdt = [(kc[x] * jnp.exp(gc_row[x][:, c - 1:c] - gc_col[x])).T.astype(BF16) for x in every]
        attn_w = [jnp.dot(attn[x], w_b[x], preferred_element_type=F32) for x in every]
        attn_u = [jnp.dot(attn[x], u_b[x], preferred_element_type=F32) for x in every]
        kd_w = [jnp.dot(kdt[x], w_b[x], preferred_element_type=F32) for x in every]
        kd_u = [jnp.dot(kdt[x], u_b[x], preferred_element_type=F32) for x in every]
        for x, (n, g) in enumerate(chains):
            qws[rows[x], cols[x]] = (qc[x] * e_col[x] - attn_w[x]).astype(BF16)
            aus[rows[x], cols[x]] = attn_u[x]
            kws[g, n] = kd_w[x].astype(BF16)
            kus[g, n] = kd_u[x]
        return carry

    lax.fori_loop(0, nchunks // chunks_per_iter, prepare_step, 0)

    def state_step(n, state):
        rows = pl.ds(pl.multiple_of(n * c, c), c)
        heads = range(heads_per_step)
        cols = [slice(g * dh, (g + 1) * dh) for g in heads]
        state_b = [s.astype(BF16) for s in state]
        decay_last = [jnp.exp(gcs[g, pl.ds(n, 1), :][:, c - 1:c]) for g in heads]
        new_state = tuple(state[g] * decay_last[g] + kus[g, n]
                          - jnp.dot(kws[g, n], state_b[g], preferred_element_type=F32) for g in heads)
        for g in heads:
            o_c = aus[rows, cols[g]] + jnp.dot(qws[rows, cols[g]], state_b[g], preferred_element_type=F32)
            o_n = o_c * lax.rsqrt(jnp.mean(o_c * o_c, axis=-1, keepdims=True) + NORM_EPS)
            o_ref[rows, cols[g]] = (o_n * nw_ref[...] * _silu(z_ref[rows, cols[g]])).astype(o_ref.dtype)
        return new_state

    lax.fori_loop(0, nchunks, state_step, tuple(jnp.zeros((dh, dh), F32) for _ in range(heads_per_step)))


def _gdn(proj, gates_t, seq, q_col, k_col, v_col, z_col, width, conv_w, a_log, dt_bias, norm_w,
         heads_per_step=2, chunks_per_iter=4):
    m = proj.shape[0]
    nb = m // seq
    gw = heads_per_step * GDN_HEAD_DIM
    nh = width // GDN_HEAD_DIM
    nchunks = seq // GDN_CHUNK
    jq, jk, jv, jz = (col // gw for col in (q_col, k_col, v_col, z_col))
    wblocks = width // gw
    gt3 = gates_t.reshape(2 * nh, m // GDN_CHUNK, GDN_CHUNK)
    smem = pl.BlockSpec(memory_space=pltpu.SMEM)

    def col_spec(j0):
        return pl.BlockSpec((seq, gw), lambda b, h: (b, j0 + h))

    def cw_spec(j0):
        return pl.BlockSpec((CONV_WIDTH, gw), lambda b, h: (0, j0 + h))

    return pl.pallas_call(
        functools.partial(_gdn_kernel, heads_per_step=heads_per_step, chunks_per_iter=chunks_per_iter),
        grid=(nb, nh // heads_per_step),
        in_specs=[smem, smem, col_spec(jq), col_spec(jk), col_spec(jv), col_spec(jz),
                  cw_spec(0), cw_spec(wblocks), cw_spec(2 * wblocks),
                  pl.BlockSpec((2 * nh, nchunks, GDN_CHUNK), lambda b, h: (0, b, 0)),
                  pl.BlockSpec((1, GDN_HEAD_DIM), lambda b, h: (0, 0))],
        out_specs=pl.BlockSpec((seq, gw), lambda b, h: (b, h)),
        out_shape=jax.ShapeDtypeStruct((m, width), BF16),
        scratch_shapes=[pltpu.VMEM((seq, gw), F32), pltpu.VMEM((seq, gw), F32), pltpu.VMEM((seq, gw), F32),
                        pltpu.VMEM((seq, gw), F32), pltpu.VMEM((seq, gw), BF16),
                        pltpu.VMEM((heads_per_step, nchunks, GDN_HEAD_DIM, GDN_HEAD_DIM), BF16),
                        pltpu.VMEM((heads_per_step, nchunks, GDN_HEAD_DIM, GDN_HEAD_DIM), F32),
                        pltpu.VMEM((heads_per_step, nchunks, 2 * GDN_CHUNK), F32),
                        pltpu.VMEM((heads_per_step, nchunks, 2 * GDN_CHUNK), F32)],
        compiler_params=_cparams("parallel", "parallel"),
        name="gdn",
    )(a_log, dt_bias, proj, proj, proj, proj, conv_w, conv_w, conv_w, gt3, norm_w.reshape(1, GDN_HEAD_DIM))


def _top_k_rows(s, k, payload=None):
    r = s.shape[0]
    rowi = lax.broadcasted_iota(I32, s.shape, 0)
    vals, picks = [], []
    for _ in range(k):
        mx = jnp.max(s, axis=0, keepdims=True)
        arg = jnp.min(jnp.where(s == mx, rowi, r), axis=0, keepdims=True)
        hit = rowi == arg
        vals.append(mx)
        if payload is None:
            picks.append(arg)
        else:
            picks.append(jnp.max(jnp.where(hit, payload, -1), axis=0, keepdims=True))
        s = jnp.where(hit, -jnp.inf, s)
    return jnp.concatenate(vals, axis=0), jnp.concatenate(picks, axis=0)


def _route_kernel(q_ref, keys_ref, u_ref, v_ref, o_ref, ub_ref, vb_ref, i1s, i2s, gts):
    tm = q_ref.shape[0]
    kk = PEER_TOPK
    ub_ref[...] = u_ref[...].astype(BF16)
    vb_ref[...] = v_ref[...].astype(BF16)
    i1_rows, i2_rows, gate_rows = [], [], []
    for h in range(PEER_HEADS):
        tops = []
        for p in range(2):
            c0 = (h * 2 + p) * PEER_HALF
            qhp = q_ref[:, c0:c0 + PEER_HALF].astype(BF16)
            s_t = lax.dot_general(keys_ref[h, p].astype(BF16), qhp, NT_DIMS,
                                  preferred_element_type=F32)
            tops.append(_top_k_rows(s_t, kk))
        (s1, i1), (s2, i2) = tops
        cand_parts, id_parts = [], []
        a = 0
        while kk // (a + 1) > 1:
            nb = kk // (a + 1)
            nbp = -(-nb // SUBLANES) * SUBLANES
            part = s1[a:a + 1, :] + s2[:nbp, :]
            if nbp > nb:
                part = jnp.where(lax.broadcasted_iota(I32, part.shape, 0) < nb, part, -jnp.inf)
            cand_parts.append(part)
            id_parts.append(i1[a:a + 1, :] * PEER_NKEYS + i2[:nbp, :])
            a += 1
        cand_parts.append(s1[a:, :] + s2[0:1, :])
        id_parts.append(i1[a:, :] * PEER_NKEYS + i2[0:1, :])
        best, ids = _top_k_rows(jnp.concatenate(cand_parts, axis=0), kk,
                                payload=jnp.concatenate(id_parts, axis=0))
        e = jnp.exp(best - best[0:1, :])
        gate_rows.append(e / jnp.sum(e, axis=0, keepdims=True))
        i1_rows.append(jnp.right_shift(ids, PEER_NKEYS.bit_length() - 1))
        i2_rows.append(jnp.bitwise_and(ids, PEER_NKEYS - 1))
    i1s[...] = jnp.concatenate(i1_rows, axis=0).T
    i2s[...] = jnp.concatenate(i2_rows, axis=0).T
    gts[...] = jnp.concatenate(gate_rows, axis=0).T

    nslots = PEER_HEADS * kk
    key_iota = lax.broadcasted_iota(I32, (PEER_NKEYS, nslots), 0)

    def group(gi):
        base = pl.multiple_of(gi * SUBLANES, SUBLANES)
        tiles = []
        for tk in range(SUBLANES):
            i1r = jnp.broadcast_to(i1s[pl.ds(base + tk, 1), :], (PEER_NKEYS, nslots))
            i2r = jnp.broadcast_to(i2s[pl.ds(base + tk, 1), :], (PEER_NKEYS, nslots))
            gr = jnp.broadcast_to(gts[pl.ds(base + tk, 1), :], (PEER_NKEYS, nslots))
            a_t = jnp.where(key_iota == i1r, gr, 0.0).astype(BF16)
            b_t = jnp.where(key_iota == i2r, 1.0, 0.0).astype(BF16)
            tiles.append(lax.dot_general(a_t, b_t, NT_DIMS, preferred_element_type=F32))
        o_ref[gi] = jnp.swapaxes(jnp.stack(tiles, axis=0), 0, 1)

    def groups(it, carry):
        for k in range(ROUTE_GROUPS_PER_ITER):
            group(it * ROUTE_GROUPS_PER_ITER + k)
        return carry

    lax.fori_loop(0, tm // (SUBLANES * ROUTE_GROUPS_PER_ITER), groups, 0)


def _route(q, keys, u_tab, v_tab, tm=128):
    m, qd = q.shape
    tm = _tile(m, tm)
    nsteps = m // tm
    ne, d = u_tab.shape
    er = ne // nsteps
    assert er * nsteps == ne and er % (2 * SUBLANES) == 0, (ne, nsteps)
    nslots = PEER_HEADS * PEER_TOPK
    table = pl.BlockSpec((er, d), lambda i: (i, 0))
    return pl.pallas_call(
        _route_kernel,
        grid=(nsteps,),
        in_specs=[pl.BlockSpec((tm, qd), lambda i: (i, 0)),
                  pl.BlockSpec(keys.shape, lambda i: (0, 0, 0, 0)),
                  table, table],
        out_specs=[pl.BlockSpec((tm // SUBLANES, PEER_NKEYS, SUBLANES, PEER_NKEYS), lambda i: (i, 0, 0, 0)),
                   table, table],
        out_shape=[jax.ShapeDtypeStruct((m // SUBLANES, PEER_NKEYS, SUBLANES, PEER_NKEYS), F32),
                   jax.ShapeDtypeStruct((ne, d), BF16), jax.ShapeDtypeStruct((ne, d), BF16)],
        scratch_shapes=[pltpu.VMEM((tm, nslots), I32), pltpu.VMEM((tm, nslots), I32),
                        pltpu.VMEM((tm, nslots), F32)],
        compiler_params=_cparams("parallel"),
        name="peer_route",
    )(q, keys, u_tab, v_tab)


def _peer_kernel(x_ref, u_ref, v_ref, g_ref, o_ref, w_scr, *, i1_per_step):
    j = pl.program_id(1)
    last = pl.num_programs(1) - 1
    tm = x_ref.shape[0]

    def hidden():
        hid = lax.dot_general(x_ref[...], u_ref[...], NT_DIMS, preferred_element_type=F32)
        parts = []
        for gi in range(i1_per_step):
            gate = g_ref[:, gi].reshape(tm, PEER_NKEYS)
            parts.append((gate * _gelu(hid[:, gi * PEER_NKEYS:(gi + 1) * PEER_NKEYS])).astype(BF16))
        return jnp.concatenate(parts, axis=1)

    @pl.when(j == 0)
    def _():
        o_ref[...] = jnp.zeros_like(o_ref)
        w_scr[...] = hidden()

    @pl.when(jnp.logical_and(j > 0, j < last))
    def _():
        w_prev = w_scr[...]
        w_scr[...] = hidden()
        o_ref[...] += jnp.dot(w_prev, v_ref[...], preferred_element_type=F32)

    @pl.when(j == last)
    def _():
        o_ref[...] += jnp.dot(w_scr[...], v_ref[...], preferred_element_type=F32)


def _peer(x1b, ub, vb, gates, tm=512, i1_per_step=4):
    m, d = x1b.shape
    ne = ub.shape[0]
    tm = _tile(m, tm)
    te = i1_per_step * PEER_NKEYS
    nblk = ne // te
    return pl.pallas_call(
        functools.partial(_peer_kernel, i1_per_step=i1_per_step),
        grid=(m // tm, nblk + 1),
        in_specs=[pl.BlockSpec((tm, d), lambda i, j: (i, 0)),
                  pl.BlockSpec((te, d), lambda i, j: (jnp.minimum(j, nblk - 1), 0)),
                  pl.BlockSpec((te, d), lambda i, j: (jnp.maximum(j - 1, 0), 0)),
                  pl.BlockSpec((tm // SUBLANES, i1_per_step, SUBLANES, PEER_NKEYS),
                               lambda i, j: (i, jnp.minimum(j, nblk - 1), 0, 0))],
        out_specs=pl.BlockSpec((tm, d), lambda i, j: (i, 0)),
        out_shape=jax.ShapeDtypeStruct((m, d), F32),
        scratch_shapes=[pltpu.VMEM((tm, te), BF16)],
        compiler_params=_cparams("parallel", "arbitrary"),
        name="peer_experts",
    )(x1b, ub, vb, gates)


def _layer(h, hb, p, seq, w_in, rg_conv_w, rg_conv_b, rg_wa, rg_ba, rg_wx, rg_bx, rg_lambda, rg_out,
           gdn_conv_w, gdn_a_log, gdn_dt_bias, gdn_norm_w, gdn_out, w_o, ln1_g, ln1_b, peer_wq, peer_keys,
           peer_u, peer_v, ple_w, ple_gate_w, ple_gate_b, ln2_g, ln2_b, alpha, last):
    d = h.shape[1]
    rg_w = rg_out.shape[0]
    gdn_w = gdn_out.shape[0]
    nh = gdn_w // GDN_HEAD_DIM
    n_main = 2 * rg_w + 4 * gdn_w
    w_in_t = w_in.T
    cols = {"rg_x": 0, "rg_y": rg_w, "q": 2 * rg_w, "k": 2 * rg_w + gdn_w, "v": 2 * rg_w + 2 * gdn_w,
            "z": 2 * rg_w + 3 * gdn_w}

    proj = _matmul_wt(hb, w_in_t, 0, n_main, tm=1024, tn=512, name="in_proj")
    proj_m = _matmul_wt(hb, w_in_t, n_main + 2 * nh, 2 * d, tm=1024, tn=512, name="in_proj_merge")
    gates_t = _matmul_nt(w_in_t, n_main, 2 * nh, hb, tm=512)

    h_a = _rglru(proj, seq, cols["rg_x"], cols["rg_y"], rg_w, rg_conv_w, rg_conv_b, rg_wa, rg_ba,
                 rg_wx, rg_bx, rg_lambda)
    o_b = _gdn(proj, gates_t, seq, cols["q"], cols["k"], cols["v"], cols["z"], gdn_w, gdn_conv_w,
               gdn_a_log, gdn_dt_bias, gdn_norm_w)
    merged = _merge(h_a, o_b, rg_out, gdn_out, proj_m, 0, d)
    mix = _matmul(merged, w_o, tm=1024, tn=512, name="out_proj")
    x1, x1b = _add_ln([h, mix], ln1_g, ln1_b, alpha, want_bf16=True)

    q = _matmul(x1b, peer_wq, tm=1024, tn=512, name="peer_query")
    gates, peer_ub, peer_vb = _route(q, peer_keys, peer_u, peer_v)
    ffn = _peer(x1b, peer_ub, peer_vb, gates)
    ple = _ple(x1b, ple_gate_w, ple_gate_b, p, ple_w)
    out = _add_ln([x1, ffn, ple], ln2_g, ln2_b, alpha, want_bf16=not last)
    return (out[0], None) if last else tuple(out)


def kernel(x, p, ln_emb_g, ln_emb_b, w_in, rg_conv_w, rg_conv_b, rg_wa, rg_ba, rg_wx, rg_bx, rg_lambda, rg_out, gdn_conv_w, gdn_a_log, gdn_dt_bias, gdn_norm_w, gdn_out, w_o, ln1_g, ln1_b, peer_wq, peer_keys, peer_u, peer_v, ple_w, ple_gate_w, ple_gate_b, ln2_g, ln2_b):
    bsz, seq, d = x.shape
    depth = w_in.shape[0]
    m = bsz * seq
    alpha = (2.0 * depth) ** 0.25
    h, hb = _add_ln([x.reshape(m, d)], ln_emb_g, ln_emb_b, 1.0, want_bf16=True)
    for i in range(depth):
        h, hb = _layer(h, hb, p[i].reshape(m, -1), seq, w_in[i], rg_conv_w[i], rg_conv_b[i], rg_wa[i], rg_ba[i],
                       rg_wx[i], rg_bx[i], rg_lambda[i], rg_out[i], gdn_conv_w[i], gdn_a_log[i], gdn_dt_bias[i],
                       gdn_norm_w[i], gdn_out[i], w_o[i], ln1_g[i], ln1_b[i], peer_wq[i], peer_keys[i],
                       peer_u[i], peer_v[i], ple_w[i], ple_gate_w[i], ple_gate_b[i], ln2_g[i], ln2_b[i], alpha, i == depth - 1)
    return h.reshape(bsz, seq, d)
```

```python
import functools
import math

import jax
import jax.numpy as jnp
from jax import lax
from jax.experimental import pallas as pl
from jax.experimental.pallas import tpu as pltpu

F32 = jnp.float32
BF16 = jnp.bfloat16
I32 = jnp.int32

LANES = 128
SUBLANES = 8
VMEM_LIMIT = 56 * 1024 * 1024

CONV_WIDTH = 4
RG_BLOCK = 128
RG_C = 8.0
GDN_HEAD_DIM = 128
GDN_CHUNK = 64
PEER_HEADS = 8
PEER_NKEYS = 128
PEER_HALF = 128
PEER_TOPK = 16
GATE_GROUP = 16
ROUTE_GROUPS_PER_ITER = 2
LN_EPS = 1e-5
NORM_EPS = 1e-6
HIGHEST = lax.Precision.HIGHEST
NT_DIMS = (((1,), (1,)), ((), ()))


def _cparams(*sem):
    return pltpu.CompilerParams(dimension_semantics=sem, vmem_limit_bytes=VMEM_LIMIT)


def _tile(n, target, *also_divides):
    return math.gcd(target, n, *also_divides)


def _gelu(x):
    return 0.5 * x * (1.0 + lax.erf(x * (2.0 ** -0.5)))


def _silu(x):
    return x * jax.nn.sigmoid(x)


def _add_ln_kernel(*refs, n_in, alpha, want_bf16):
    ins = refs[:n_in]
    g_ref, b_ref = refs[n_in], refs[n_in + 1]
    outs = refs[n_in + 2:]
    x = ins[0][...]
    if alpha != 1.0:
        x = alpha * x
    for r in ins[1:]:
        x = x + r[...]
    mu = jnp.mean(x, axis=-1, keepdims=True)
    xc = x - mu
    var = jnp.mean(xc * xc, axis=-1, keepdims=True)
    y = xc * lax.rsqrt(var + LN_EPS) * g_ref[...] + b_ref[...]
    outs[0][...] = y
    if want_bf16:
        outs[1][...] = y.astype(BF16)


def _add_ln(ins, g, b, alpha, want_bf16, tm=256):
    m, d = ins[0].shape
    tm = _tile(m, tm)
    row = pl.BlockSpec((tm, d), lambda i: (i, 0))
    vec = pl.BlockSpec((1, d), lambda i: (0, 0))
    out_shape = [jax.ShapeDtypeStruct((m, d), F32)]
    if want_bf16:
        out_shape.append(jax.ShapeDtypeStruct((m, d), BF16))
    return pl.pallas_call(
        functools.partial(_add_ln_kernel, n_in=len(ins), alpha=alpha, want_bf16=want_bf16),
        grid=(m // tm,),
        in_specs=[row] * len(ins) + [vec, vec],
        out_specs=[row] * len(out_shape),
        out_shape=out_shape,
        compiler_params=_cparams("parallel"),
        name="add_ln",
    )(*ins, g.reshape(1, d), b.reshape(1, d))


def _mm_kernel(a_ref, b_ref, o_ref):
    o_ref[...] = jnp.dot(a_ref[...], b_ref[...].astype(BF16), preferred_element_type=F32).astype(o_ref.dtype)


def _matmul(a, b, tm, tn, n=None, out_dtype=F32, name="matmul"):
    m, k = a.shape
    n = b.shape[1] if n is None else n
    tm, tn = _tile(m, tm), _tile(n, tn)
    return pl.pallas_call(
        _mm_kernel,
        grid=(m // tm, n // tn),
        in_specs=[pl.BlockSpec((tm, k), lambda i, j: (i, 0)),
                  pl.BlockSpec((k, tn), lambda i, j: (0, j))],
        out_specs=pl.BlockSpec((tm, tn), lambda i, j: (i, j)),
        out_shape=jax.ShapeDtypeStruct((m, n), out_dtype),
        compiler_params=_cparams("parallel", "arbitrary"),
        name=name,
    )(a, b)


def _mm_wt_kernel(a_ref, wt_ref, o_ref):
    o_ref[...] = lax.dot_general(a_ref[...], wt_ref[...].astype(BF16), NT_DIMS, preferred_element_type=F32)


def _matmul_wt(a, wt, row0, n, tm, tn, name):
    m, k = a.shape
    tm, tn = _tile(m, tm), _tile(n, tn)
    return pl.pallas_call(
        _mm_wt_kernel,
        grid=(m // tm, n // tn),
        in_specs=[pl.BlockSpec((tm, k), lambda i, j: (i, 0)),
                  pl.BlockSpec((pl.Element(tn), pl.Element(k)),
                               lambda i, j: (pl.multiple_of(row0 + j * tn, SUBLANES), 0))],
        out_specs=pl.BlockSpec((tm, tn), lambda i, j: (i, j)),
        out_shape=jax.ShapeDtypeStruct((m, n), F32),
        compiler_params=_cparams("parallel", "arbitrary"),
        name=name,
    )(a, wt)


def _mm_nt_kernel(w_ref, a_ref, o_ref):
    o_ref[...] = lax.dot_general(w_ref[...].astype(BF16), a_ref[...], NT_DIMS, preferred_element_type=F32)


def _matmul_nt(wt, row0, r, a, tm):
    k = wt.shape[1]
    m = a.shape[0]
    tm = _tile(m, tm)
    return pl.pallas_call(
        _mm_nt_kernel,
        grid=(m // tm,),
        in_specs=[pl.BlockSpec((pl.Element(r), pl.Element(k)), lambda i: (row0, 0)),
                  pl.BlockSpec((tm, k), lambda i: (i, 0))],
        out_specs=pl.BlockSpec((r, tm), lambda i: (0, i)),
        out_shape=jax.ShapeDtypeStruct((r, m), F32),
        compiler_params=_cparams("parallel"),
        name="gate_proj",
    )(wt, a)


def _merge_kernel(ha_ref, ob_ref, wa_ref, wb_ref, ma_ref, mb_ref, o_ref):
    ya = jnp.dot(ha_ref[...], wa_ref[...].astype(BF16), preferred_element_type=F32)
    yb = jnp.dot(ob_ref[...], wb_ref[...].astype(BF16), preferred_element_type=F32)
    o_ref[...] = (jax.nn.sigmoid(ma_ref[...]) * ya + jax.nn.sigmoid(mb_ref[...]) * yb).astype(o_ref.dtype)


def _merge(ha, ob, wa, wb, proj, ma_col, mb_col, tm=1024, tn=512):
    m, k = ha.shape
    n = wa.shape[1]
    tm, tn = _tile(m, tm), _tile(n, tn, ma_col, mb_col)
    ja, jb = ma_col // tn, mb_col // tn
    return pl.pallas_call(
        _merge_kernel,
        grid=(m // tm, n // tn),
        in_specs=[pl.BlockSpec((tm, k), lambda i, j: (i, 0)),
                  pl.BlockSpec((tm, k), lambda i, j: (i, 0)),
                  pl.BlockSpec((k, tn), lambda i, j: (0, j)),
                  pl.BlockSpec((k, tn), lambda i, j: (0, j)),
                  pl.BlockSpec((tm, tn), lambda i, j: (i, ja + j)),
                  pl.BlockSpec((tm, tn), lambda i, j: (i, jb + j))],
        out_specs=pl.BlockSpec((tm, tn), lambda i, j: (i, j)),
        out_shape=jax.ShapeDtypeStruct((m, n), BF16),
        compiler_params=_cparams("parallel", "arbitrary"),
        name="merge",
    )(ha, ob, wa, wb, proj, proj)


def _causal_conv(x, w, row):
    y = x * w[CONV_WIDTH - 1:CONV_WIDTH, :]
    for j in range(CONV_WIDTH - 1):
        s = CONV_WIDTH - 1 - j
        y = y + jnp.where(row >= s, pltpu.roll(x, s, 0), 0.0) * w[j:j + 1, :]
    return y


def _rglru_kernel(x_ref, y_ref, cw_ref, cb_ref, wa_ref, ba_ref, wx_ref, bx_ref, lam_ref, o_ref):
    t, c = x_ref.shape
    row = lax.broadcasted_iota(I32, (t, c), 0)
    xr = _causal_conv(x_ref[...], cw_ref[...], row) + cb_ref[...]
    xb = xr.astype(BF16)
    rs, is_ = [], []
    for k in range(c // RG_BLOCK):
        blk = xb[:, k * RG_BLOCK:(k + 1) * RG_BLOCK]
        rs.append(jnp.dot(blk, wa_ref[k], preferred_element_type=F32))
        is_.append(jnp.dot(blk, wx_ref[k], preferred_element_type=F32))
    r = jax.nn.sigmoid(jnp.concatenate(rs, axis=1) + ba_ref[...])
    i = jax.nn.sigmoid(jnp.concatenate(is_, axis=1) + bx_ref[...])
    log_a = (-RG_C) * r * jax.nn.softplus(-lam_ref[...])
    a = jnp.exp(log_a)
    b = jnp.sqrt(-jnp.tanh(log_a) * (a * a + 1.0)) * (i * xr)
    d = 1
    while d < t:
        keep = row >= d
        a_s = jnp.where(keep, pltpu.roll(a, d, 0), 1.0)
        b_s = jnp.where(keep, pltpu.roll(b, d, 0), 0.0)
        b = a * b_s + b
        a = a * a_s
        d *= 2
    o_ref[...] = (b * _gelu(y_ref[...])).astype(o_ref.dtype)


def _rglru(proj, seq, x_col, y_col, width, cw, cb, wa, ba, wx, bx, lam, ct=256):
    m = proj.shape[0]
    nb = m // seq
    ct = _tile(width, ct, x_col, y_col)
    jx, jy = x_col // ct, y_col // ct
    kb = ct // RG_BLOCK
    vec = pl.BlockSpec((1, ct), lambda b, c: (0, c))
    gate_w = pl.BlockSpec((kb, RG_BLOCK, RG_BLOCK), lambda b, c: (c, 0, 0))
    return pl.pallas_call(
        _rglru_kernel,
        grid=(nb, width // ct),
        in_specs=[pl.BlockSpec((seq, ct), lambda b, c: (b, jx + c)),
                  pl.BlockSpec((seq, ct), lambda b, c: (b, jy + c)),
                  pl.BlockSpec((CONV_WIDTH, ct), lambda b, c: (0, c)),
                  vec, gate_w, vec, gate_w, vec, vec],
        out_specs=pl.BlockSpec((seq, ct), lambda b, c: (b, c)),
        out_shape=jax.ShapeDtypeStruct((m, width), BF16),
        compiler_params=_cparams("parallel", "parallel"),
        name="rglru",
    )(proj, proj, cw, cb.reshape(1, width), wa.astype(BF16), ba.reshape(1, width),
      wx.astype(BF16), bx.reshape(1, width), lam.reshape(1, width))


def _dot_hi(a, b):
    return jnp.dot(a, b, precision=HIGHEST, preferred_element_type=F32)


def _split_bf16(x):
    hi = x.astype(BF16)
    return hi, x - hi.astype(F32)


def _lhs3(x_dup, first):
    _, lo = _split_bf16(x_dup)
    return jnp.concatenate([jnp.where(first, x_dup, lo).astype(BF16),
                            jnp.where(first, x_dup, 0.0).astype(BF16)], axis=1)


def _rhs3(p):
    hi, lo = _split_bf16(p)
    return jnp.concatenate([hi, hi, lo.astype(BF16), jnp.zeros_like(hi)], axis=0)


def _dot3(lhs3, rhs3):
    return jnp.dot(lhs3, rhs3, preferred_element_type=F32)


def _gdn_kernel(alog_ref, dtb_ref, q_ref, k_ref, v_ref, z_ref, cwq_ref, cwk_ref, cwv_ref, gt_ref, nw_ref,
                o_ref, qs, ks, vs, aus, qws, kws, kus, gcs, bts, *, heads_per_step, chunks_per_iter):
    t, gw = q_ref.shape
    c = GDN_CHUNK
    dh = GDN_HEAD_DIM
    nchunks = t // c
    hg = pl.program_id(1)
    nheads = gt_ref.shape[0] // 2

    row = lax.broadcasted_iota(I32, (t, gw), 0)
    q = _silu(_causal_conv(q_ref[...], cwq_ref[...], row))
    k = _silu(_causal_conv(k_ref[...], cwk_ref[...], row))
    vs[...] = _silu(_causal_conv(v_ref[...], cwv_ref[...], row))
    for g in range(heads_per_step):
        sl = slice(g * dh, (g + 1) * dh)
        qh, kh = q[:, sl], k[:, sl]
        qs[:, sl] = qh * lax.rsqrt(jnp.sum(qh * qh, axis=-1, keepdims=True) + NORM_EPS) * (dh ** -0.5)
        ks[:, sl] = kh * lax.rsqrt(jnp.sum(kh * kh, axis=-1, keepdims=True) + NORM_EPS)

    ri = lax.broadcasted_iota(I32, (c, 2 * c), 0)
    lane = lax.broadcasted_iota(I32, (c, 2 * c), 1)
    ci = jnp.bitwise_and(lane, c - 1)
    first = lane < c
    first2 = lax.broadcasted_iota(I32, (2 * c, 2 * c), 1) < c
    eye_first = ri == lane
    causal = ri >= ci
    strict = ri > ci
    eye_f = (ri == ci).astype(F32)
    upper_f = (ri <= ci).astype(F32)

    for g in range(heads_per_step):
        h = hg * heads_per_step + g
        a_gate = gt_ref[nheads + h]
        b_gate = gt_ref[h]
        neg_rate = -jnp.exp(jnp.zeros_like(a_gate) + alog_ref[h])
        g_log = neg_rate * jax.nn.softplus(a_gate + dtb_ref[h])
        gcs[g] = _dot_hi(g_log, upper_f)
        bts[g] = _dot_hi(jax.nn.sigmoid(b_gate), eye_f)

    def to_col(row_vec):
        return jnp.sum(jnp.where(eye_first, jnp.broadcast_to(row_vec, (c, 2 * c)), 0.0), axis=1, keepdims=True)

    def prepare_step(i, carry):
        chains = [(i * chunks_per_iter + j, g) for j in range(chunks_per_iter) for g in range(heads_per_step)]
        rows = [pl.ds(pl.multiple_of(n * c, c), c) for n, _ in chains]
        cols = [slice(g * dh, (g + 1) * dh) for _, g in chains]
        every = range(len(chains))
        gc_row = [gcs[g, pl.ds(n, 1), :] for n, g in chains]
        gc_col = [to_col(r) for r in gc_row]
        beta_col = [to_col(bts[g, pl.ds(n, 1), :]) for n, g in chains]
        decay = [jnp.where(causal, jnp.exp(jnp.where(causal, gc_col[x] - gc_row[x], 0.0)), 0.0) for x in every]
        e_col = [jnp.exp(cv) for cv in gc_col]
        qc = [qs[rows[x], cols[x]] for x in every]
        kc = [ks[rows[x], cols[x]] for x in every]
        vc = [vs[rows[x], cols[x]] for x in every]
        kb = [kc[x] * beta_col[x] for x in every]

        kk = []
        k2_hi = []
        for x in every:
            kb_hi, kb_lo = _split_bf16(kb[x])
            hi, lo = _split_bf16(jnp.concatenate([kc[x], kc[x]], axis=0))
            k2_hi.append(hi)
            kk.append(lax.dot_general(jnp.concatenate([kb_hi, kb_lo.astype(BF16), kb_hi], axis=1),
                                      jnp.concatenate([hi, hi, lo.astype(BF16)], axis=1),
                                      NT_DIMS, preferred_element_type=F32))
        a_low = [jnp.where(strict, kk[x] * decay[x], 0.0) for x in every]
        inv = [eye_f - a for a in a_low]
        pw = [_dot3(_lhs3(a, first), _rhs3(a)) for a in a_low]
        span = 2
        while 2 * span < c:
            both = [_dot3(_lhs3(jnp.concatenate([inv[x], pw[x]], axis=0), first2), _rhs3(pw[x])) for x in every]
            inv = [inv[x] + both[x][:c] for x in every]
            pw = [b[c:] for b in both]
            span *= 2
        inv = [inv[x] + _dot3(_lhs3(inv[x], first), _rhs3(pw[x])) for x in every]
        sol = [_dot3(_lhs3(inv[x], first),
                     _rhs3(jnp.concatenate([vc[x] * beta_col[x], kb[x] * e_col[x]], axis=1))) for x in every]
        attn = [(lax.dot_general(qc[x].astype(BF16), k2_hi[x], NT_DIMS, preferred_element_type=F32)
                 * decay[x])[:, :c].astype(BF16) for x in every]
        u_b = [s[:, :dh].astype(BF16) for s in sol]
        w_b = [s[:, dh:].astype(BF16) for s in sol]
        kdt = [(kc[x] * jnp.exp(gc_row[x][:, c - 1:c] - gc_col[x])).T.astype(BF16) for x in every]
        attn_w = [jnp.dot(attn[x], w_b[x], preferred_element_type=F32) for x in every]
        attn_u = [jnp.dot(attn[x], u_b[x], preferred_element_type=F32) for x in every]
        kd_w = [jnp.dot(kdt[x], w_b[x], preferred_element_type=F32) for x in every]
        kd_u = [jnp.dot(kdt[x], u_b[x], preferred_element_type=F32) for x in every]
        for x, (n, g) in enumerate(chains):
            qws[rows[x], cols[x]] = (qc[x] * e_col[x] - attn_w[x]).astype(BF16)
            aus[rows[x], cols[x]] = attn_u[x]
            kws[g, n] = kd_w[x].astype(BF16)
            kus[g, n] = kd_u[x]
        return carry

    lax.fori_loop(0, nchunks // chunks_per_iter, prepare_step, 0)

    def state_step(n, state):
        rows = pl.ds(pl.multiple_of(n * c, c), c)
        heads = range(heads_per_step)
        cols = [slice(g * dh, (g + 1) * dh) for g in heads]
        state_b = [s.astype(BF16) for s in state]
        decay_last = [jnp.exp(gcs[g, pl.ds(n, 1), :][:, c - 1:c]) for g in heads]
        new_state = tuple(state[g] * decay_last[g] + kus[g, n]
                          - jnp.dot(kws[g, n], state_b[g], preferred_element_type=F32) for g in heads)
        for g in heads:
            o_c = aus[rows, cols[g]] + jnp.dot(qws[rows, cols[g]], state_b[g], preferred_element_type=F32)
            o_n = o_c * lax.rsqrt(jnp.mean(o_c * o_c, axis=-1, keepdims=True) + NORM_EPS)
            o_ref[rows, cols[g]] = (o_n * nw_ref[...] * _silu(z_ref[rows, cols[g]])).astype(o_ref.dtype)
        return new_state

    lax.fori_loop(0, nchunks, state_step, tuple(jnp.zeros((dh, dh), F32) for _ in range(heads_per_step)))


def _gdn(proj, gates_t, seq, q_col, k_col, v_col, z_col, width, conv_w, a_log, dt_bias, norm_w,
         heads_per_step=2, chunks_per_iter=8):
    m = proj.shape[0]
    nb = m // seq
    gw = heads_per_step * GDN_HEAD_DIM
    nh = width // GDN_HEAD_DIM
    nchunks = seq // GDN_CHUNK
    chunks_per_iter = math.gcd(chunks_per_iter, nchunks)
    jq, jk, jv, jz = (col // gw for col in (q_col, k_col, v_col, z_col))
    wblocks = width // gw
    gt3 = gates_t.reshape(2 * nh, m // GDN_CHUNK, GDN_CHUNK)
    smem = pl.BlockSpec(memory_space=pltpu.SMEM)

    def col_spec(j0):
        return pl.BlockSpec((seq, gw), lambda b, h: (b, j0 + h))

    def cw_spec(j0):
        return pl.BlockSpec((CONV_WIDTH, gw), lambda b, h: (0, j0 + h))

    return pl.pallas_call(
        functools.partial(_gdn_kernel, heads_per_step=heads_per_step, chunks_per_iter=chunks_per_iter),
        grid=(nb, nh // heads_per_step),
        in_specs=[smem, smem, col_spec(jq), col_spec(jk), col_spec(jv), col_spec(jz),
                  cw_spec(0), cw_spec(wblocks), cw_spec(2 * wblocks),
                  pl.BlockSpec((2 * nh, nchunks, GDN_CHUNK), lambda b, h: (0, b, 0)),
                  pl.BlockSpec((1, GDN_HEAD_DIM), lambda b, h: (0, 0))],
        out_specs=pl.BlockSpec((seq, gw), lambda b, h: (b, h)),
        out_shape=jax.ShapeDtypeStruct((m, width), BF16),
        scratch_shapes=[pltpu.VMEM((seq, gw), F32), pltpu.VMEM((seq, gw), F32), pltpu.VMEM((seq, gw), F32),
                        pltpu.VMEM((seq, gw), F32), pltpu.VMEM((seq, gw), BF16),
                        pltpu.VMEM((heads_per_step, nchunks, GDN_HEAD_DIM, GDN_HEAD_DIM), BF16),
                        pltpu.VMEM((heads_per_step, nchunks, GDN_HEAD_DIM, GDN_HEAD_DIM), F32),
                        pltpu.VMEM((heads_per_step, nchunks, 2 * GDN_CHUNK), F32),
                        pltpu.VMEM((heads_per_step, nchunks, 2 * GDN_CHUNK), F32)],
        compiler_params=_cparams("parallel", "parallel"),
        name="gdn",
    )(a_log, dt_bias, proj, proj, proj, proj, conv_w, conv_w, conv_w, gt3, norm_w.reshape(1, GDN_HEAD_DIM))


def _top_k_rows(s, k, payload=None):
    r = s.shape[0]
    rowi = lax.broadcasted_iota(I32, s.shape, 0)
    vals, picks = [], []
    for _ in range(k):
        mx = jnp.max(s, axis=0, keepdims=True)
        arg = jnp.min(jnp.where(s == mx, rowi, r), axis=0, keepdims=True)
        hit = rowi == arg
        vals.append(mx)
        if payload is None:
            picks.append(arg)
        else:
            picks.append(jnp.max(jnp.where(hit, payload, -1), axis=0, keepdims=True))
        s = jnp.where(hit, -jnp.inf, s)
    return jnp.concatenate(vals, axis=0), jnp.concatenate(picks, axis=0)


def _select_experts(s1_t, s2_t):
    kk = PEER_TOPK
    s1, i1 = _top_k_rows(s1_t, kk)
    s2, i2 = _top_k_rows(s2_t, kk)
    cand_parts, id_parts = [], []
    a = 0
    while kk // (a + 1) > 1:
        nb = kk // (a + 1)
        nbp = -(-nb // SUBLANES) * SUBLANES
        part = s1[a:a + 1, :] + s2[:nbp, :]
        if nbp > nb:
            part = jnp.where(lax.broadcasted_iota(I32, part.shape, 0) < nb, part, -jnp.inf)
        cand_parts.append(part)
        id_parts.append(i1[a:a + 1, :] * PEER_NKEYS + i2[:nbp, :])
        a += 1
    cand_parts.append(s1[a:, :] + s2[0:1, :])
    id_parts.append(i1[a:, :] * PEER_NKEYS + i2[0:1, :])
    best, ids = _top_k_rows(jnp.concatenate(cand_parts, axis=0), kk,
                            payload=jnp.concatenate(id_parts, axis=0))
    e = jnp.exp(best - best[0:1, :])
    gates = e / jnp.sum(e, axis=0, keepdims=True)
    return jnp.right_shift(ids, PEER_NKEYS.bit_length() - 1), jnp.bitwise_and(ids, PEER_NKEYS - 1), gates


def _ple_route_kernel(x_ref, wg_ref, bg_ref, p_ref, wp_ref, q_ref, keys_ref, o_ref, i1_ref, i2_ref, gt_ref):
    k = x_ref.shape[1]
    kc = k // PEER_HEADS
    rt = q_ref.shape[0]
    acc = None
    for h in range(PEER_HEADS):
        ks = slice(h * kc, (h + 1) * kc)
        part = jnp.dot(x_ref[:, ks], wg_ref[ks, :].astype(BF16), preferred_element_type=F32)
        acc = part if acc is None else acc + part
        rows = slice(h * PEER_TOPK, (h + 1) * PEER_TOPK)
        for lg in range(rt // LANES):
            tok = slice(lg * LANES, (lg + 1) * LANES)
            scores = []
            for p in range(2):
                c0 = (h * 2 + p) * PEER_HALF
                scores.append(lax.dot_general(keys_ref[h, p].astype(BF16),
                                              q_ref[tok, c0:c0 + PEER_HALF].astype(BF16),
                                              NT_DIMS, preferred_element_type=F32))
            i1, i2, gates = _select_experts(*scores)
            i1_ref[rows, tok] = i1
            i2_ref[rows, tok] = i2
            gt_ref[rows, tok] = gates
    gate = jax.nn.sigmoid(acc + bg_ref[...])
    o_ref[...] = gate * jnp.dot(p_ref[...].astype(BF16), wp_ref[...].astype(BF16), preferred_element_type=F32)


def _ple_route(x1b, wg, bg, p, wp, q, keys, tm=1024, tn=512):
    m, k = x1b.shape
    n = wg.shape[1]
    kp = p.shape[1]
    qd = q.shape[1]
    tm, tn = _tile(m, tm), _tile(n, tn)
    nj = n // tn
    rt = m // ((m // tm) * nj)
    assert rt % LANES == 0 and k % PEER_HEADS == 0, (rt, k)
    nslots = PEER_HEADS * PEER_TOPK
    table = pl.BlockSpec((nslots, rt), lambda i, j: (0, i * nj + j))
    return pl.pallas_call(
        _ple_route_kernel,
        grid=(m // tm, nj),
        in_specs=[pl.BlockSpec((tm, k), lambda i, j: (i, 0)),
                  pl.BlockSpec((k, tn), lambda i, j: (0, j)),
                  pl.BlockSpec((1, tn), lambda i, j: (0, j)),
                  pl.BlockSpec((tm, kp), lambda i, j: (i, 0)),
                  pl.BlockSpec((kp, tn), lambda i, j: (0, j)),
                  pl.BlockSpec((rt, qd), lambda i, j: (i * nj + j, 0)),
                  pl.BlockSpec(keys.shape, lambda i, j: (0, 0, 0, 0))],
        out_specs=[pl.BlockSpec((tm, tn), lambda i, j: (i, j)), table, table, table],
        out_shape=[jax.ShapeDtypeStruct((m, n), F32), jax.ShapeDtypeStruct((nslots, m), I32),
                   jax.ShapeDtypeStruct((nslots, m), I32), jax.ShapeDtypeStruct((nslots, m), F32)],
        compiler_params=_cparams("parallel", "arbitrary"),
        name="ple_route",
    )(x1b, wg, bg.reshape(1, n), p, wp, q, keys)


def _gates_kernel(i1t_ref, i2t_ref, gtt_ref, u_ref, v_ref, o_ref, ub_ref, vb_ref, i1s, i2s, gts):
    tm = i1t_ref.shape[1]
    kk = PEER_TOPK
    ub_ref[...] = u_ref[...].astype(BF16)
    vb_ref[...] = v_ref[...].astype(BF16)
    i1s[...] = i1t_ref[...].T
    i2s[...] = i2t_ref[...].T
    gts[...] = gtt_ref[...].T

    nslots = PEER_HEADS * kk
    key_iota = lax.broadcasted_iota(I32, (PEER_NKEYS, nslots), 0)

    def group(gi):
        base = pl.multiple_of(gi * GATE_GROUP, GATE_GROUP)
        tiles = []
        for tk in range(GATE_GROUP):
            i1r = jnp.broadcast_to(i1s[pl.ds(base + tk, 1), :], (PEER_NKEYS, nslots))
            i2r = jnp.broadcast_to(i2s[pl.ds(base + tk, 1), :], (PEER_NKEYS, nslots))
            gr = jnp.broadcast_to(gts[pl.ds(base + tk, 1), :], (PEER_NKEYS, nslots))
            a_t = jnp.where(key_iota == i1r, gr, 0.0).astype(BF16)
            b_t = jnp.where(key_iota == i2r, 1.0, 0.0).astype(BF16)
            tiles.append(lax.dot_general(a_t, b_t, NT_DIMS, preferred_element_type=F32))
        o_ref[gi] = jnp.swapaxes(jnp.stack(tiles, axis=0), 0, 1).astype(o_ref.dtype)

    def groups(it, carry):
        for k in range(ROUTE_GROUPS_PER_ITER):
            group(it * ROUTE_GROUPS_PER_ITER + k)
        return carry

    lax.fori_loop(0, tm // (GATE_GROUP * ROUTE_GROUPS_PER_ITER), groups, 0)


def _gates(i1t, i2t, gtt, u_tab, v_tab, tm=128):
    nslots, m = i1t.shape
    tm = _tile(m, tm)
    nsteps = m // tm
    ne, d = u_tab.shape
    er = ne // nsteps
    assert er * nsteps == ne and er % (2 * SUBLANES) == 0, (ne, nsteps)
    table = pl.BlockSpec((er, d), lambda i: (i, 0))
    slots = pl.BlockSpec((nslots, tm), lambda i: (0, i))
    return pl.pallas_call(
        _gates_kernel,
        grid=(nsteps,),
        in_specs=[slots, slots, slots, table, table],
        out_specs=[pl.BlockSpec((tm // GATE_GROUP, PEER_NKEYS, GATE_GROUP, PEER_NKEYS), lambda i: (i, 0, 0, 0)),
                   table, table],
        out_shape=[jax.ShapeDtypeStruct((m // GATE_GROUP, PEER_NKEYS, GATE_GROUP, PEER_NKEYS), BF16),
                   jax.ShapeDtypeStruct((ne, d), BF16), jax.ShapeDtypeStruct((ne, d), BF16)],
        scratch_shapes=[pltpu.VMEM((tm, nslots), I32), pltpu.VMEM((tm, nslots), I32),
                        pltpu.VMEM((tm, nslots), F32)],
        compiler_params=_cparams("parallel"),
        name="peer_gates",
    )(i1t, i2t, gtt, u_tab, v_tab)


def _peer_kernel(x_ref, u_ref, v_ref, g_ref, o_ref, w_scr, *, i1_per_step, nblk):
    s = pl.program_id(0)
    last = pl.num_programs(0) - 1
    tm = x_ref.shape[0]
    opens_tile = (s - 1) % nblk == 0

    def hidden():
        hid = lax.dot_general(x_ref[...], u_ref[...], NT_DIMS, preferred_element_type=F32)
        parts = []
        for gi in range(i1_per_step):
            gate = g_ref[:, gi].reshape(tm, PEER_NKEYS).astype(F32)
            parts.append((gate * _gelu(hid[:, gi * PEER_NKEYS:(gi + 1) * PEER_NKEYS])).astype(BF16))
        return jnp.concatenate(parts, axis=1)

    @pl.when(s == 0)
    def _():
        w_scr[...] = hidden()

    def both(first_of_tile):
        w_prev = w_scr[...]
        w_scr[...] = hidden()
        contrib = jnp.dot(w_prev, v_ref[...], preferred_element_type=F32)
        if first_of_tile:
            o_ref[...] = contrib
        else:
            o_ref[...] += contrib

    inner = jnp.logical_and(s > 0, s < last)
    pl.when(jnp.logical_and(inner, opens_tile))(functools.partial(both, True))
    pl.when(jnp.logical_and(inner, jnp.logical_not(opens_tile)))(functools.partial(both, False))

    @pl.when(s == last)
    def _():
        o_ref[...] += jnp.dot(w_scr[...], v_ref[...], preferred_element_type=F32)


def _peer(x1b, ub, vb, gates, tm=512, i1_per_step=4):
    m, d = x1b.shape
    ne = ub.shape[0]
    tm = _tile(m, tm)
    te = i1_per_step * PEER_NKEYS
    nblk = ne // te
    assert nblk > 1, "the last step accumulates: a token tile needs more than one expert block"
    npairs = (m // tm) * nblk

    def pair(s):
        sc = jnp.minimum(s, npairs - 1)
        return sc // nblk, sc % nblk

    def prev_pair(s):
        sp = jnp.maximum(s - 1, 0)
        return sp // nblk, sp % nblk

    return pl.pallas_call(
        functools.partial(_peer_kernel, i1_per_step=i1_per_step, nblk=nblk),
        grid=(npairs + 1,),
        in_specs=[pl.BlockSpec((tm, d), lambda s: (pair(s)[0], 0)),
                  pl.BlockSpec((te, d), lambda s: (pair(s)[1], 0)),
                  pl.BlockSpec((te, d), lambda s: (prev_pair(s)[1], 0)),
                  pl.BlockSpec((tm // GATE_GROUP, i1_per_step, GATE_GROUP, PEER_NKEYS),
                               lambda s: (pair(s)[0], pair(s)[1], 0, 0))],
        out_specs=pl.BlockSpec((tm, d), lambda s: (prev_pair(s)[0], 0)),
        out_shape=jax.ShapeDtypeStruct((m, d), F32),
        scratch_shapes=[pltpu.VMEM((tm, te), BF16)],
        compiler_params=_cparams("arbitrary"),
        name="peer_experts",
    )(x1b, ub, vb, gates)


def _layer(h, hb, p, seq, w_in, rg_conv_w, rg_conv_b, rg_wa, rg_ba, rg_wx, rg_bx, rg_lambda, rg_out,
           gdn_conv_w, gdn_a_log, gdn_dt_bias, gdn_norm_w, gdn_out, w_o, ln1_g, ln1_b, peer_wq, peer_keys,
           peer_u, peer_v, ple_w, ple_gate_w, ple_gate_b, ln2_g, ln2_b, alpha, last):
    d = h.shape[1]
    rg_w = rg_out.shape[0]
    gdn_w = gdn_out.shape[0]
    nh = gdn_w // GDN_HEAD_DIM
    n_main = 2 * rg_w + 4 * gdn_w
    w_in_t = w_in.T
    cols = {"rg_x": 0, "rg_y": rg_w, "q": 2 * rg_w, "k": 2 * rg_w + gdn_w, "v": 2 * rg_w + 2 * gdn_w,
            "z": 2 * rg_w + 3 * gdn_w}

    proj = _matmul_wt(hb, w_in_t, 0, n_main, tm=1024, tn=512, name="in_proj")
    proj_m = _matmul_wt(hb, w_in_t, n_main + 2 * nh, 2 * d, tm=1024, tn=512, name="in_proj_merge")
    gates_t = _matmul_nt(w_in_t, n_main, 2 * nh, hb, tm=512)

    h_a = _rglru(proj, seq, cols["rg_x"], cols["rg_y"], rg_w, rg_conv_w, rg_conv_b, rg_wa, rg_ba,
                 rg_wx, rg_bx, rg_lambda)
    o_b = _gdn(proj, gates_t, seq, cols["q"], cols["k"], cols["v"], cols["z"], gdn_w, gdn_conv_w,
               gdn_a_log, gdn_dt_bias, gdn_norm_w)
    merged = _merge(h_a, o_b, rg_out, gdn_out, proj_m, 0, d)
    mix = _matmul(merged, w_o, tm=1024, tn=512, name="out_proj")
    x1, x1b = _add_ln([h, mix], ln1_g, ln1_b, alpha, want_bf16=True)

    q = _matmul(x1b, peer_wq, tm=1024, tn=512, name="peer_query")
    ple, i1t, i2t, gtt = _ple_route(x1b, ple_gate_w, ple_gate_b, p, ple_w, q, peer_keys)
    gates, peer_ub, peer_vb = _gates(i1t, i2t, gtt, peer_u, peer_v)
    ffn = _peer(x1b, peer_ub, peer_vb, gates)
    out = _add_ln([x1, ffn, ple], ln2_g, ln2_b, alpha, want_bf16=not last)
    return (out[0], None) if last else tuple(out)


def kernel(x, p, ln_emb_g, ln_emb_b, w_in, rg_conv_w, rg_conv_b, rg_wa, rg_ba, rg_wx, rg_bx, rg_lambda, rg_out, gdn_conv_w, gdn_a_log, gdn_dt_bias, gdn_norm_w, gdn_out, w_o, ln1_g, ln1_b, peer_wq, peer_keys, peer_u, peer_v, ple_w, ple_gate_w, ple_gate_b, ln2_g, ln2_b):
    bsz, seq, d = x.shape
    depth = w_in.shape[0]
    m = bsz * seq
    alpha = (2.0 * depth) ** 0.25
    h, hb = _add_ln([x.reshape(m, d)], ln_emb_g, ln_emb_b, 1.0, want_bf16=True)
    for i in range(depth):
        h, hb = _layer(h, hb, p[i].reshape(m, -1), seq, w_in[i], rg_conv_w[i], rg_conv_b[i], rg_wa[i], rg_ba[i],
                       rg_wx[i], rg_bx[i], rg_lambda[i], rg_out[i], gdn_conv_w[i], gdn_a_log[i], gdn_dt_bias[i],
                       gdn_norm_w[i], gdn_out[i], w_o[i], ln1_g[i], ln1_b[i], peer_wq[i], peer_keys[i],
                       peer_u[i], peer_v[i], ple_w[i], ple_gate_w[i], ple_gate_b[i], ln2_g[i], ln2_b[i], alpha, i == depth - 1)
    return h.reshape(bsz, seq, d)
```

```python
import functools
import math

import jax
import jax.numpy as jnp
from jax import lax
from jax.experimental import pallas as pl
from jax.experimental.pallas import tpu as pltpu

F32 = jnp.float32
BF16 = jnp.bfloat16
I32 = jnp.int32

LANES = 128
SUBLANES = 8
VMEM_LIMIT = 56 * 1024 * 1024

CONV_WIDTH = 4
RG_BLOCK = 128
RG_C = 8.0
GDN_HEAD_DIM = 128
GDN_CHUNK = 64
PEER_HEADS = 8
PEER_NKEYS = 128
PEER_HALF = 128
PEER_TOPK = 16
GATE_GROUP = 16
ROUTE_GROUPS_PER_ITER = 2
LN_EPS = 1e-5
NORM_EPS = 1e-6
HIGHEST = lax.Precision.HIGHEST
NT_DIMS = (((1,), (1,)), ((), ()))


def _cparams(*sem):
    return pltpu.CompilerParams(dimension_semantics=sem, vmem_limit_bytes=VMEM_LIMIT)


def _tile(n, target, *also_divides):
    return math.gcd(target, n, *also_divides)


def _gelu(x):
    return 0.5 * x * (1.0 + lax.erf(x * (2.0 ** -0.5)))


def _silu(x):
    return x * jax.nn.sigmoid(x)


def _layer_norm(x, g, b):
    mu = jnp.mean(x, axis=-1, keepdims=True)
    xc = x - mu
    var = jnp.mean(xc * xc, axis=-1, keepdims=True)
    return xc * lax.rsqrt(var + LN_EPS) * g + b


def _add_ln_kernel(*refs, n_in, alpha, entry, out_dtypes):
    ins = refs[:n_in]
    params = refs[n_in:len(refs) - len(out_dtypes)]
    outs = refs[len(refs) - len(out_dtypes):]
    x = ins[0][...]
    if entry:
        x = _layer_norm(x, params[0][...], params[1][...])
    if alpha != 1.0:
        x = alpha * x
    for r in ins[1:]:
        x = x + r[...]
    y = _layer_norm(x, params[-2][...], params[-1][...])
    for o_ref in outs:
        o_ref[...] = y.astype(o_ref.dtype)


def _add_ln(ins, g, b, alpha, out_dtypes, entry=None, tm=256):
    m, d = ins[0].shape
    tm = _tile(m, tm)
    row = pl.BlockSpec((tm, d), lambda i: (i, 0))
    vec = pl.BlockSpec((1, d), lambda i: (0, 0))
    params = [t.reshape(1, d) for t in (tuple(entry) if entry else ()) + (g, b)]
    return pl.pallas_call(
        functools.partial(_add_ln_kernel, n_in=len(ins), alpha=alpha, entry=bool(entry),
                          out_dtypes=tuple(out_dtypes)),
        grid=(m // tm,),
        in_specs=[row] * len(ins) + [vec] * len(params),
        out_specs=[row] * len(out_dtypes),
        out_shape=[jax.ShapeDtypeStruct((m, d), dt) for dt in out_dtypes],
        compiler_params=_cparams("parallel"),
        name="add_ln",
    )(*ins, *params)


def _mm_kernel(a_ref, b_ref, o_ref):
    o_ref[...] = jnp.dot(a_ref[...], b_ref[...].astype(BF16), preferred_element_type=F32).astype(o_ref.dtype)


def _matmul(a, b, tm, tn, n=None, out_dtype=F32, name="matmul"):
    m, k = a.shape
    n = b.shape[1] if n is None else n
    tm, tn = _tile(m, tm), _tile(n, tn)
    return pl.pallas_call(
        _mm_kernel,
        grid=(m // tm, n // tn),
        in_specs=[pl.BlockSpec((tm, k), lambda i, j: (i, 0)),
                  pl.BlockSpec((k, tn), lambda i, j: (0, j))],
        out_specs=pl.BlockSpec((tm, tn), lambda i, j: (i, j)),
        out_shape=jax.ShapeDtypeStruct((m, n), out_dtype),
        compiler_params=_cparams("parallel", "arbitrary"),
        name=name,
    )(a, b)


def _mm_wt_kernel(a_ref, wt_ref, o_ref):
    o_ref[...] = lax.dot_general(a_ref[...], wt_ref[...].astype(BF16), NT_DIMS, preferred_element_type=F32)


def _matmul_wt(a, wt, row0, n, tm, tn, name):
    m, k = a.shape
    tm, tn = _tile(m, tm), _tile(n, tn)
    return pl.pallas_call(
        _mm_wt_kernel,
        grid=(m // tm, n // tn),
        in_specs=[pl.BlockSpec((tm, k), lambda i, j: (i, 0)),
                  pl.BlockSpec((pl.Element(tn), pl.Element(k)),
                               lambda i, j: (pl.multiple_of(row0 + j * tn, SUBLANES), 0))],
        out_specs=pl.BlockSpec((tm, tn), lambda i, j: (i, j)),
        out_shape=jax.ShapeDtypeStruct((m, n), F32),
        compiler_params=_cparams("parallel", "arbitrary"),
        name=name,
    )(a, wt)


def _mm_nt_kernel(w_ref, a_ref, o_ref):
    o_ref[...] = lax.dot_general(w_ref[...].astype(BF16), a_ref[...], NT_DIMS, preferred_element_type=F32)


def _matmul_nt(wt, row0, r, a, tm):
    k = wt.shape[1]
    m = a.shape[0]
    tm = _tile(m, tm)
    return pl.pallas_call(
        _mm_nt_kernel,
        grid=(m // tm,),
        in_specs=[pl.BlockSpec((pl.Element(r), pl.Element(k)), lambda i: (row0, 0)),
                  pl.BlockSpec((tm, k), lambda i: (i, 0))],
        out_specs=pl.BlockSpec((r, tm), lambda i: (0, i)),
        out_shape=jax.ShapeDtypeStruct((r, m), F32),
        compiler_params=_cparams("parallel"),
        name="gate_proj",
    )(wt, a)


def _merge_kernel(ha_ref, ob_ref, wa_ref, wb_ref, ma_ref, mb_ref, o_ref):
    ya = jnp.dot(ha_ref[...], wa_ref[...].astype(BF16), preferred_element_type=F32)
    yb = jnp.dot(ob_ref[...], wb_ref[...].astype(BF16), preferred_element_type=F32)
    o_ref[...] = (jax.nn.sigmoid(ma_ref[...]) * ya + jax.nn.sigmoid(mb_ref[...]) * yb).astype(o_ref.dtype)


def _merge(ha, ob, wa, wb, proj, ma_col, mb_col, tm=1024, tn=512):
    m, k = ha.shape
    n = wa.shape[1]
    tm, tn = _tile(m, tm), _tile(n, tn, ma_col, mb_col)
    ja, jb = ma_col // tn, mb_col // tn
    return pl.pallas_call(
        _merge_kernel,
        grid=(m // tm, n // tn),
        in_specs=[pl.BlockSpec((tm, k), lambda i, j: (i, 0)),
                  pl.BlockSpec((tm, k), lambda i, j: (i, 0)),
                  pl.BlockSpec((k, tn), lambda i, j: (0, j)),
                  pl.BlockSpec((k, tn), lambda i, j: (0, j)),
                  pl.BlockSpec((tm, tn), lambda i, j: (i, ja + j)),
                  pl.BlockSpec((tm, tn), lambda i, j: (i, jb + j))],
        out_specs=pl.BlockSpec((tm, tn), lambda i, j: (i, j)),
        out_shape=jax.ShapeDtypeStruct((m, n), BF16),
        compiler_params=_cparams("parallel", "arbitrary"),
        name="merge",
    )(ha, ob, wa, wb, proj, proj)


def _causal_conv(x, w):
    def taps(v, history):
        y = v * w[CONV_WIDTH - 1:CONV_WIDTH, :]
        for j in range(CONV_WIDTH - 1):
            y = y + history(v, CONV_WIDTH - 1 - j) * w[j:j + 1, :]
        return y

    head = x[:SUBLANES, :]
    head_row = lax.broadcasted_iota(I32, head.shape, 0)
    y_head = taps(head, lambda v, s: jnp.where(head_row >= s, pltpu.roll(v, s, 0), 0.0))
    y = taps(x, lambda v, s: pltpu.roll(v, s, 0))
    return jnp.concatenate([y_head, y[SUBLANES:, :]], axis=0)


def _rglru_kernel(x_ref, y_ref, cw_ref, cb_ref, wa_ref, ba_ref, wx_ref, bx_ref, lam_ref, o_ref):
    t, c = x_ref.shape
    row = lax.broadcasted_iota(I32, (t, c), 0)
    xr = _causal_conv(x_ref[...], cw_ref[...]) + cb_ref[...]
    xb = xr.astype(BF16)
    rs, is_ = [], []
    for k in range(c // RG_BLOCK):
        blk = xb[:, k * RG_BLOCK:(k + 1) * RG_BLOCK]
        rs.append(jnp.dot(blk, wa_ref[k], preferred_element_type=F32))
        is_.append(jnp.dot(blk, wx_ref[k], preferred_element_type=F32))
    r = jax.nn.sigmoid(jnp.concatenate(rs, axis=1) + ba_ref[...])
    i = jax.nn.sigmoid(jnp.concatenate(is_, axis=1) + bx_ref[...])
    log_a = (-RG_C) * r * jax.nn.softplus(-lam_ref[...])
    a = jnp.exp(log_a)
    b = jnp.sqrt(-jnp.tanh(log_a) * (a * a + 1.0)) * (i * xr)
    d = 1
    while d < t:
        keep = row >= d
        a_s = jnp.where(keep, pltpu.roll(a, d, 0), 1.0)
        b_s = jnp.where(keep, pltpu.roll(b, d, 0), 0.0)
        b = a * b_s + b
        a = a * a_s
        d *= 2
    o_ref[...] = (b * _gelu(y_ref[...])).astype(o_ref.dtype)


def _rglru(proj, seq, x_col, y_col, width, cw, cb, wa, ba, wx, bx, lam, ct=256):
    m = proj.shape[0]
    nb = m // seq
    ct = _tile(width, ct, x_col, y_col)
    jx, jy = x_col // ct, y_col // ct
    kb = ct // RG_BLOCK
    vec = pl.BlockSpec((1, ct), lambda b, c: (0, c))
    gate_w = pl.BlockSpec((kb, RG_BLOCK, RG_BLOCK), lambda b, c: (c, 0, 0))
    return pl.pallas_call(
        _rglru_kernel,
        grid=(nb, width // ct),
        in_specs=[pl.BlockSpec((seq, ct), lambda b, c: (b, jx + c)),
                  pl.BlockSpec((seq, ct), lambda b, c: (b, jy + c)),
                  pl.BlockSpec((CONV_WIDTH, ct), lambda b, c: (0, c)),
                  vec, gate_w, vec, gate_w, vec, vec],
        out_specs=pl.BlockSpec((seq, ct), lambda b, c: (b, c)),
        out_shape=jax.ShapeDtypeStruct((m, width), BF16),
        compiler_params=_cparams("parallel", "parallel"),
        name="rglru",
    )(proj, proj, cw, cb.reshape(1, width), wa.astype(BF16), ba.reshape(1, width),
      wx.astype(BF16), bx.reshape(1, width), lam.reshape(1, width))


def _dot_hi(a, b):
    return jnp.dot(a, b, precision=HIGHEST, preferred_element_type=F32)


def _split_bf16(x):
    hi = x.astype(BF16)
    return hi, x - hi.astype(F32)


def _lhs3(x_dup, first):
    _, lo = _split_bf16(x_dup)
    return jnp.concatenate([jnp.where(first, x_dup, lo).astype(BF16),
                            jnp.where(first, x_dup, 0.0).astype(BF16)], axis=1)


def _rhs3(p):
    hi, lo = _split_bf16(p)
    return jnp.concatenate([hi, hi, lo.astype(BF16), jnp.zeros_like(hi)], axis=0)


def _dot3(lhs3, rhs3):
    return jnp.dot(lhs3, rhs3, preferred_element_type=F32)


def _gdn_kernel(alog_ref, dtb_ref, q_ref, k_ref, v_ref, z_ref, cwq_ref, cwk_ref, cwv_ref, gt_ref, nw_ref,
                o_ref, qs, ks, vs, aus, qws, kws, kus, gcs, bts, *, heads_per_step, chunks_per_iter):
    t, gw = q_ref.shape
    c = GDN_CHUNK
    dh = GDN_HEAD_DIM
    nchunks = t // c
    hg = pl.program_id(1)
    nheads = gt_ref.shape[0] // 2

    q = _silu(_causal_conv(q_ref[...], cwq_ref[...]))
    k = _silu(_causal_conv(k_ref[...], cwk_ref[...]))
    vs[...] = _silu(_causal_conv(v_ref[...], cwv_ref[...]))
    for g in range(heads_per_step):
        sl = slice(g * dh, (g + 1) * dh)
        qh, kh = q[:, sl], k[:, sl]
        qs[:, sl] = qh * lax.rsqrt(jnp.sum(qh * qh, axis=-1, keepdims=True) + NORM_EPS) * (dh ** -0.5)
        ks[:, sl] = kh * lax.rsqrt(jnp.sum(kh * kh, axis=-1, keepdims=True) + NORM_EPS)

    ri = lax.broadcasted_iota(I32, (c, 2 * c), 0)
    lane = lax.broadcasted_iota(I32, (c, 2 * c), 1)
    ci = jnp.bitwise_and(lane, c - 1)
    first = lane < c
    first2 = lax.broadcasted_iota(I32, (2 * c, 2 * c), 1) < c
    eye_first = ri == lane
    causal = ri >= ci
    strict = ri > ci
    eye_f = (ri == ci).astype(F32)
    upper_f = (ri <= ci).astype(F32)

    for g in range(heads_per_step):
        h = hg * heads_per_step + g
        a_gate = gt_ref[nheads + h]
        b_gate = gt_ref[h]
        neg_rate = -jnp.exp(jnp.zeros_like(a_gate) + alog_ref[h])
        g_log = neg_rate * jax.nn.softplus(a_gate + dtb_ref[h])
        gcs[g] = _dot_hi(g_log, upper_f)
        bts[g] = _dot_hi(jax.nn.sigmoid(b_gate), eye_f)

    def to_col(row_vec):
        return jnp.sum(jnp.where(eye_first, jnp.broadcast_to(row_vec, (c, 2 * c)), 0.0), axis=1, keepdims=True)

    def prepare_step(i, carry):
        chains = [(i * chunks_per_iter + j, g) for j in range(chunks_per_iter) for g in range(heads_per_step)]
        rows = [pl.ds(pl.multiple_of(n * c, c), c) for n, _ in chains]
        cols = [slice(g * dh, (g + 1) * dh) for _, g in chains]
        every = range(len(chains))
        gc_row = [gcs[g, pl.ds(n, 1), :] for n, g in chains]
        gc_col = [to_col(r) for r in gc_row]
        beta_col = [to_col(bts[g, pl.ds(n, 1), :]) for n, g in chains]
        decay = [jnp.where(causal, jnp.exp(jnp.where(causal, gc_col[x] - gc_row[x], 0.0)), 0.0) for x in every]
        e_col = [jnp.exp(cv) for cv in gc_col]
        qc = [qs[rows[x], cols[x]] for x in every]
        kc = [ks[rows[x], cols[x]] for x in every]
        vc = [vs[rows[x], cols[x]] for x in every]
        kb = [kc[x] * beta_col[x] for x in every]

        kk = []
        k2_hi = []
        for x in every:
            kb_hi, kb_lo = _split_bf16(kb[x])
            hi, lo = _split_bf16(jnp.concatenate([kc[x], kc[x]], axis=0))
            k2_hi.append(hi)
            kk.append(lax.dot_general(jnp.concatenate([kb_hi, kb_lo.astype(BF16), kb_hi], axis=1),
                                      jnp.concatenate([hi, hi, lo.astype(BF16)], axis=1),
                                      NT_DIMS, preferred_element_type=F32))
        a_low = [jnp.where(strict, kk[x] * decay[x], 0.0) for x in every]
        inv = [eye_f - a for a in a_low]
        pw = [_dot3(_lhs3(a, first), _rhs3(a)) for a in a_low]
        span = 2
        while 2 * span < c:
            both = [_dot3(_lhs3(jnp.concatenate([inv[x], pw[x]], axis=0), first2), _rhs3(pw[x])) for x in every]
            inv = [inv[x] + both[x][:c] for x in every]
            pw = [b[c:] for b in both]
            span *= 2
        inv = [inv[x] + _dot3(_lhs3(inv[x], first), _rhs3(pw[x])) for x in every]
        sol = [_dot3(_lhs3(inv[x], first),
                     _rhs3(jnp.concatenate([vc[x] * beta_col[x], kb[x] * e_col[x]], axis=1))) for x in every]
        attn = [(lax.dot_general(qc[x].astype(BF16), k2_hi[x], NT_DIMS, preferred_element_type=F32)
                 * decay[x])[:, :c].astype(BF16) for x in every]
        u_b = [s[:, :dh].astype(BF16) for s in sol]
        w_b = [s[:, dh:].astype(BF16) for s in sol]
        kdt = [(kc[x] * jnp.exp(gc_row[x][:, c - 1:c] - gc_col[x])).T.astype(BF16) for x in every]
        attn_w = [jnp.dot(attn[x], w_b[x], preferred_element_type=F32) for x in every]
        attn_u = [jnp.dot(attn[x], u_b[x], preferred_element_type=F32) for x in every]
        kd_w = [jnp.dot(kdt[x], w_b[x], preferred_element_type=F32) for x in every]
        kd_u = [jnp.dot(kdt[x], u_b[x], preferred_element_type=F32) for x in every]
        for x, (n, g) in enumerate(chains):
            qws[rows[x], cols[x]] = (qc[x] * e_col[x] - attn_w[x]).astype(BF16)
            aus[rows[x], cols[x]] = attn_u[x]
            kws[g, n] = kd_w[x].astype(BF16)
            kus[g, n] = kd_u[x]
        return carry

    lax.fori_loop(0, nchunks // chunks_per_iter, prepare_step, 0)

    def state_step(n, state):
        rows = pl.ds(pl.multiple_of(n * c, c), c)
        heads = range(heads_per_step)
        cols = [slice(g * dh, (g + 1) * dh) for g in heads]
        state_b = [s.astype(BF16) for s in state]
        decay_last = [jnp.exp(gcs[g, pl.ds(n, 1), :][:, c - 1:c]) for g in heads]
        new_state = tuple(state[g] * decay_last[g] + kus[g, n]
                          - jnp.dot(kws[g, n], state_b[g], preferred_element_type=F32) for g in heads)
        for g in heads:
            o_c = aus[rows, cols[g]] + jnp.dot(qws[rows, cols[g]], state_b[g], preferred_element_type=F32)
            o_n = o_c * lax.rsqrt(jnp.mean(o_c * o_c, axis=-1, keepdims=True) + NORM_EPS)
            o_ref[rows, cols[g]] = (o_n * nw_ref[...] * _silu(z_ref[rows, cols[g]])).astype(o_ref.dtype)
        return new_state

    lax.fori_loop(0, nchunks, state_step, tuple(jnp.zeros((dh, dh), F32) for _ in range(heads_per_step)))


def _gdn(proj, gates_t, seq, q_col, k_col, v_col, z_col, width, conv_w, a_log, dt_bias, norm_w,
         heads_per_step=2, chunks_per_iter=8):
    m = proj.shape[0]
    nb = m // seq
    gw = heads_per_step * GDN_HEAD_DIM
    nh = width // GDN_HEAD_DIM
    nchunks = seq // GDN_CHUNK
    chunks_per_iter = math.gcd(chunks_per_iter, nchunks)
    jq, jk, jv, jz = (col // gw for col in (q_col, k_col, v_col, z_col))
    wblocks = width // gw
    gt3 = gates_t.reshape(2 * nh, m // GDN_CHUNK, GDN_CHUNK)
    smem = pl.BlockSpec(memory_space=pltpu.SMEM)

    def col_spec(j0):
        return pl.BlockSpec((seq, gw), lambda b, h: (b, j0 + h))

    def cw_spec(j0):
        return pl.BlockSpec((CONV_WIDTH, gw), lambda b, h: (0, j0 + h))

    return pl.pallas_call(
        functools.partial(_gdn_kernel, heads_per_step=heads_per_step, chunks_per_iter=chunks_per_iter),
        grid=(nb, nh // heads_per_step),
        in_specs=[smem, smem, col_spec(jq), col_spec(jk), col_spec(jv), col_spec(jz),
                  cw_spec(0), cw_spec(wblocks), cw_spec(2 * wblocks),
                  pl.BlockSpec((2 * nh, nchunks, GDN_CHUNK), lambda b, h: (0, b, 0)),
                  pl.BlockSpec((1, GDN_HEAD_DIM), lambda b, h: (0, 0))],
        out_specs=pl.BlockSpec((seq, gw), lambda b, h: (b, h)),
        out_shape=jax.ShapeDtypeStruct((m, width), BF16),
        scratch_shapes=[pltpu.VMEM((seq, gw), F32), pltpu.VMEM((seq, gw), F32), pltpu.VMEM((seq, gw), F32),
                        pltpu.VMEM((seq, gw), F32), pltpu.VMEM((seq, gw), BF16),
                        pltpu.VMEM((heads_per_step, nchunks, GDN_HEAD_DIM, GDN_HEAD_DIM), BF16),
                        pltpu.VMEM((heads_per_step, nchunks, GDN_HEAD_DIM, GDN_HEAD_DIM), F32),
                        pltpu.VMEM((heads_per_step, nchunks, 2 * GDN_CHUNK), F32),
                        pltpu.VMEM((heads_per_step, nchunks, 2 * GDN_CHUNK), F32)],
        compiler_params=_cparams("parallel", "parallel"),
        name="gdn",
    )(a_log, dt_bias, proj, proj, proj, proj, conv_w, conv_w, conv_w, gt3, norm_w.reshape(1, GDN_HEAD_DIM))


def _top_k_rows(s, k, payload=None):
    r = s.shape[0]
    rowi = lax.broadcasted_iota(I32, s.shape, 0)
    vals, picks = [], []
    for _ in range(k):
        mx = jnp.max(s, axis=0, keepdims=True)
        arg = jnp.min(jnp.where(s == mx, rowi, r), axis=0, keepdims=True)
        hit = rowi == arg
        vals.append(mx)
        if payload is None:
            picks.append(arg)
        else:
            picks.append(jnp.max(jnp.where(hit, payload, -1), axis=0, keepdims=True))
        s = jnp.where(hit, -jnp.inf, s)
    return jnp.concatenate(vals, axis=0), jnp.concatenate(picks, axis=0)


def _select_experts(s1_t, s2_t):
    kk = PEER_TOPK
    s1, i1 = _top_k_rows(s1_t, kk)
    s2, i2 = _top_k_rows(s2_t, kk)
    cand_parts, id_parts = [], []
    a = 0
    while kk // (a + 1) > 1:
        nb = kk // (a + 1)
        nbp = -(-nb // SUBLANES) * SUBLANES
        part = s1[a:a + 1, :] + s2[:nbp, :]
        if nbp > nb:
            part = jnp.where(lax.broadcasted_iota(I32, part.shape, 0) < nb, part, -jnp.inf)
        cand_parts.append(part)
        id_parts.append(i1[a:a + 1, :] * PEER_NKEYS + i2[:nbp, :])
        a += 1
    cand_parts.append(s1[a:, :] + s2[0:1, :])
    id_parts.append(i1[a:, :] * PEER_NKEYS + i2[0:1, :])
    best, ids = _top_k_rows(jnp.concatenate(cand_parts, axis=0), kk,
                            payload=jnp.concatenate(id_parts, axis=0))
    e = jnp.exp(best - best[0:1, :])
    gates = e / jnp.sum(e, axis=0, keepdims=True)
    return jnp.right_shift(ids, PEER_NKEYS.bit_length() - 1), jnp.bitwise_and(ids, PEER_NKEYS - 1), gates


def _ple_route_kernel(x_ref, wg_ref, bg_ref, p_ref, wp_ref, q_ref, keys_ref, o_ref, i1_ref, i2_ref, gt_ref):
    k = x_ref.shape[1]
    kc = k // PEER_HEADS
    rt = q_ref.shape[0]
    acc = None
    for h in range(PEER_HEADS):
        ks = slice(h * kc, (h + 1) * kc)
        part = jnp.dot(x_ref[:, ks], wg_ref[ks, :].astype(BF16), preferred_element_type=F32)
        acc = part if acc is None else acc + part
        rows = slice(h * PEER_TOPK, (h + 1) * PEER_TOPK)
        for lg in range(rt // LANES):
            tok = slice(lg * LANES, (lg + 1) * LANES)
            scores = []
            for p in range(2):
                c0 = (h * 2 + p) * PEER_HALF
                scores.append(lax.dot_general(keys_ref[h, p].astype(BF16),
                                              q_ref[tok, c0:c0 + PEER_HALF].astype(BF16),
                                              NT_DIMS, preferred_element_type=F32))
            i1, i2, gates = _select_experts(*scores)
            i1_ref[rows, tok] = i1
            i2_ref[rows, tok] = i2
            gt_ref[rows, tok] = gates
    gate = jax.nn.sigmoid(acc + bg_ref[...])
    o_ref[...] = gate * jnp.dot(p_ref[...].astype(BF16), wp_ref[...].astype(BF16), preferred_element_type=F32)


def _ple_route(x1b, wg, bg, p, wp, q, keys, tm=1024, tn=512):
    m, k = x1b.shape
    n = wg.shape[1]
    kp = p.shape[1]
    qd = q.shape[1]
    tm, tn = _tile(m, tm), _tile(n, tn)
    nj = n // tn
    rt = m // ((m // tm) * nj)
    assert rt % LANES == 0 and k % PEER_HEADS == 0, (rt, k)
    nslots = PEER_HEADS * PEER_TOPK
    table = pl.BlockSpec((nslots, rt), lambda i, j: (0, i * nj + j))
    return pl.pallas_call(
        _ple_route_kernel,
        grid=(m // tm, nj),
        in_specs=[pl.BlockSpec((tm, k), lambda i, j: (i, 0)),
                  pl.BlockSpec((k, tn), lambda i, j: (0, j)),
                  pl.BlockSpec((1, tn), lambda i, j: (0, j)),
                  pl.BlockSpec((tm, kp), lambda i, j: (i, 0)),
                  pl.BlockSpec((kp, tn), lambda i, j: (0, j)),
                  pl.BlockSpec((rt, qd), lambda i, j: (i * nj + j, 0)),
                  pl.BlockSpec(keys.shape, lambda i, j: (0, 0, 0, 0))],
        out_specs=[pl.BlockSpec((tm, tn), lambda i, j: (i, j)), table, table, table],
        out_shape=[jax.ShapeDtypeStruct((m, n), F32), jax.ShapeDtypeStruct((nslots, m), I32),
                   jax.ShapeDtypeStruct((nslots, m), I32), jax.ShapeDtypeStruct((nslots, m), F32)],
        compiler_params=_cparams("parallel", "arbitrary"),
        name="ple_route",
    )(x1b, wg, bg.reshape(1, n), p, wp, q, keys)


def _gates_kernel(i1t_ref, i2t_ref, gtt_ref, u_ref, o_ref, ub_ref, i1s, i2s, gts):
    tm = i1t_ref.shape[1]
    kk = PEER_TOPK
    ub_ref[...] = u_ref[...].astype(BF16)
    i1s[...] = i1t_ref[...].T
    i2s[...] = i2t_ref[...].T
    gts[...] = gtt_ref[...].T

    nslots = PEER_HEADS * kk
    key_iota = lax.broadcasted_iota(I32, (PEER_NKEYS, nslots), 0)

    def group(gi):
        base = pl.multiple_of(gi * GATE_GROUP, GATE_GROUP)
        tiles = []
        for tk in range(GATE_GROUP):
            i1r = jnp.broadcast_to(i1s[pl.ds(base + tk, 1), :], (PEER_NKEYS, nslots))
            i2r = jnp.broadcast_to(i2s[pl.ds(base + tk, 1), :], (PEER_NKEYS, nslots))
            gr = jnp.broadcast_to(gts[pl.ds(base + tk, 1), :], (PEER_NKEYS, nslots))
            a_t = jnp.where(key_iota == i1r, gr, 0.0).astype(BF16)
            b_t = jnp.where(key_iota == i2r, 1.0, 0.0).astype(BF16)
            tiles.append(lax.dot_general(a_t, b_t, NT_DIMS, preferred_element_type=F32))
        o_ref[gi] = jnp.swapaxes(jnp.stack(tiles, axis=0), 0, 1).astype(o_ref.dtype)

    def groups(it, carry):
        for k in range(ROUTE_GROUPS_PER_ITER):
            group(it * ROUTE_GROUPS_PER_ITER + k)
        return carry

    lax.fori_loop(0, tm // (GATE_GROUP * ROUTE_GROUPS_PER_ITER), groups, 0)


def _gates(i1t, i2t, gtt, u_tab, tm=128):
    nslots, m = i1t.shape
    tm = _tile(m, tm)
    nsteps = m // tm
    ne, d = u_tab.shape
    er = ne // nsteps
    assert er * nsteps == ne and er % (2 * SUBLANES) == 0, (ne, nsteps)
    table = pl.BlockSpec((er, d), lambda i: (i, 0))
    slots = pl.BlockSpec((nslots, tm), lambda i: (0, i))
    return pl.pallas_call(
        _gates_kernel,
        grid=(nsteps,),
        in_specs=[slots, slots, slots, table],
        out_specs=[pl.BlockSpec((tm // GATE_GROUP, PEER_NKEYS, GATE_GROUP, PEER_NKEYS), lambda i: (i, 0, 0, 0)),
                   table],
        out_shape=[jax.ShapeDtypeStruct((m // GATE_GROUP, PEER_NKEYS, GATE_GROUP, PEER_NKEYS), BF16),
                   jax.ShapeDtypeStruct((ne, d), BF16)],
        scratch_shapes=[pltpu.VMEM((tm, nslots), I32), pltpu.VMEM((tm, nslots), I32),
                        pltpu.VMEM((tm, nslots), F32)],
        compiler_params=_cparams("parallel"),
        name="peer_gates",
    )(i1t, i2t, gtt, u_tab)


def _peer_kernel(x_ref, u_ref, v_ref, g_ref, o_ref, w_scr, *, i1_per_step, nblk):
    s = pl.program_id(0)
    last = pl.num_programs(0) - 1
    tm = x_ref.shape[0]
    opens_tile = (s - 1) % nblk == 0

    def hidden():
        hid = lax.dot_general(x_ref[...], u_ref[...], NT_DIMS, preferred_element_type=F32)
        parts = []
        for gi in range(i1_per_step):
            gate = g_ref[:, gi].reshape(tm, PEER_NKEYS).astype(F32)
            parts.append((gate * _gelu(hid[:, gi * PEER_NKEYS:(gi + 1) * PEER_NKEYS])).astype(BF16))
        return jnp.concatenate(parts, axis=1)

    @pl.when(s == 0)
    def _():
        w_scr[...] = hidden()

    def both(first_of_tile):
        w_prev = w_scr[...]
        w_scr[...] = hidden()
        contrib = jnp.dot(w_prev, v_ref[...].astype(BF16), preferred_element_type=F32)
        if first_of_tile:
            o_ref[...] = contrib
        else:
            o_ref[...] += contrib

    inner = jnp.logical_and(s > 0, s < last)
    pl.when(jnp.logical_and(inner, opens_tile))(functools.partial(both, True))
    pl.when(jnp.logical_and(inner, jnp.logical_not(opens_tile)))(functools.partial(both, False))

    @pl.when(s == last)
    def _():
        o_ref[...] += jnp.dot(w_scr[...], v_ref[...].astype(BF16), preferred_element_type=F32)


def _peer(x1b, ub, vb, gates, tm=512, i1_per_step=4):
    m, d = x1b.shape
    ne = ub.shape[0]
    tm = _tile(m, tm)
    te = i1_per_step * PEER_NKEYS
    nblk = ne // te
    assert nblk > 1, "the last step accumulates: a token tile needs more than one expert block"
    npairs = (m // tm) * nblk

    def pair(s):
        sc = jnp.minimum(s, npairs - 1)
        return sc // nblk, sc % nblk

    def prev_pair(s):
        sp = jnp.maximum(s - 1, 0)
        return sp // nblk, sp % nblk

    return pl.pallas_call(
        functools.partial(_peer_kernel, i1_per_step=i1_per_step, nblk=nblk),
        grid=(npairs + 1,),
        in_specs=[pl.BlockSpec((tm, d), lambda s: (pair(s)[0], 0)),
                  pl.BlockSpec((te, d), lambda s: (pair(s)[1], 0)),
                  pl.BlockSpec((te, d), lambda s: (prev_pair(s)[1], 0)),
                  pl.BlockSpec((tm // GATE_GROUP, i1_per_step, GATE_GROUP, PEER_NKEYS),
                               lambda s: (pair(s)[0], pair(s)[1], 0, 0))],
        out_specs=pl.BlockSpec((tm, d), lambda s: (prev_pair(s)[0], 0)),
        out_shape=jax.ShapeDtypeStruct((m, d), F32),
        scratch_shapes=[pltpu.VMEM((tm, te), BF16)],
        compiler_params=_cparams("arbitrary"),
        name="peer_experts",
    )(x1b, ub, vb, gates)


def _layer(h, hb, p, seq, w_in, rg_conv_w, rg_conv_b, rg_wa, rg_ba, rg_wx, rg_bx, rg_lambda, rg_out,
           gdn_conv_w, gdn_a_log, gdn_dt_bias, gdn_norm_w, gdn_out, w_o, ln1_g, ln1_b, peer_wq, peer_keys,
           peer_u, peer_v, ple_w, ple_gate_w, ple_gate_b, ln2_g, ln2_b, alpha, last, entry):
    d = h.shape[1]
    rg_w = rg_out.shape[0]
    gdn_w = gdn_out.shape[0]
    nh = gdn_w // GDN_HEAD_DIM
    n_main = 2 * rg_w + 4 * gdn_w
    w_in_t = w_in.T
    cols = {"rg_x": 0, "rg_y": rg_w, "q": 2 * rg_w, "k": 2 * rg_w + gdn_w, "v": 2 * rg_w + 2 * gdn_w,
            "z": 2 * rg_w + 3 * gdn_w}

    proj = _matmul_wt(hb, w_in_t, 0, n_main, tm=1024, tn=512, name="in_proj")
    proj_m = _matmul_wt(hb, w_in_t, n_main + 2 * nh, 2 * d, tm=1024, tn=512, name="in_proj_merge")
    gates_t = _matmul_nt(w_in_t, n_main, 2 * nh, hb, tm=512)

    h_a = _rglru(proj, seq, cols["rg_x"], cols["rg_y"], rg_w, rg_conv_w, rg_conv_b, rg_wa, rg_ba,
                 rg_wx, rg_bx, rg_lambda)
    o_b = _gdn(proj, gates_t, seq, cols["q"], cols["k"], cols["v"], cols["z"], gdn_w, gdn_conv_w,
               gdn_a_log, gdn_dt_bias, gdn_norm_w)
    merged = _merge(h_a, o_b, rg_out, gdn_out, proj_m, 0, d)
    mix = _matmul(merged, w_o, tm=1024, tn=512, name="out_proj")
    x1, x1b = _add_ln([h, mix], ln1_g, ln1_b, alpha, (F32, BF16), entry=entry)

    q = _matmul(x1b, peer_wq, tm=1024, tn=512, name="peer_query")
    ple, i1t, i2t, gtt = _ple_route(x1b, ple_gate_w, ple_gate_b, p, ple_w, q, peer_keys)
    gates, peer_ub = _gates(i1t, i2t, gtt, peer_u)
    ffn = _peer(x1b, peer_ub, peer_v, gates)
    out = _add_ln([x1, ffn, ple], ln2_g, ln2_b, alpha, (F32,) if last else (F32, BF16))
    return (out[0], None) if last else tuple(out)


def kernel(x, p, ln_emb_g, ln_emb_b, w_in, rg_conv_w, rg_conv_b, rg_wa, rg_ba, rg_wx, rg_bx, rg_lambda, rg_out, gdn_conv_w, gdn_a_log, gdn_dt_bias, gdn_norm_w, gdn_out, w_o, ln1_g, ln1_b, peer_wq, peer_keys, peer_u, peer_v, ple_w, ple_gate_w, ple_gate_b, ln2_g, ln2_b):
    bsz, seq, d = x.shape
    depth = w_in.shape[0]
    m = bsz * seq
    alpha = (2.0 * depth) ** 0.25
    h = x.reshape(m, d)
    entry = (ln_emb_g, ln_emb_b)
    (hb,) = _add_ln([h], ln_emb_g, ln_emb_b, 1.0, (BF16,))
    for i in range(depth):
        h, hb = _layer(h, hb, p[i].reshape(m, -1), seq, w_in[i], rg_conv_w[i], rg_conv_b[i], rg_wa[i], rg_ba[i],
                       rg_wx[i], rg_bx[i], rg_lambda[i], rg_out[i], gdn_conv_w[i], gdn_a_log[i], gdn_dt_bias[i],
                       gdn_norm_w[i], gdn_out[i], w_o[i], ln1_g[i], ln1_b[i], peer_wq[i], peer_keys[i],
                       peer_u[i], peer_v[i], ple_w[i], ple_gate_w[i], ple_gate_b[i], ln2_g[i], ln2_b[i], alpha,
                       i == depth - 1, entry)
        entry = None
    return h.reshape(bsz, seq, d)
```

```python
import functools
import math

import jax
import jax.numpy as jnp
from jax import lax
from jax.experimental import pallas as pl
from jax.experimental.pallas import tpu as pltpu

F32 = jnp.float32
BF16 = jnp.bfloat16
I32 = jnp.int32

LANES = 128
SUBLANES = 8
VMEM_LIMIT = 56 * 1024 * 1024

CONV_WIDTH = 4
RG_BLOCK = 128
RG_C = 8.0
GDN_HEAD_DIM = 128
GDN_CHUNK = 64
PEER_HEADS = 8
PEER_NKEYS = 128
PEER_HALF = 128
PEER_TOPK = 16
GATE_GROUP = 16
ROUTE_GROUPS_PER_ITER = 2
LN_EPS = 1e-5
NORM_EPS = 1e-6
HIGHEST = lax.Precision.HIGHEST
NT_DIMS = (((1,), (1,)), ((), ()))


def _cparams(*sem):
    return pltpu.CompilerParams(dimension_semantics=sem, vmem_limit_bytes=VMEM_LIMIT)


def _tile(n, target, *also_divides):
    return math.gcd(target, n, *also_divides)


def _gelu(x):
    return 0.5 * x * (1.0 + lax.erf(x * (2.0 ** -0.5)))


def _silu(x):
    return x * jax.nn.sigmoid(x)


def _layer_norm(x, g, b):
    mu = jnp.mean(x, axis=-1, keepdims=True)
    xc = x - mu
    var = jnp.mean(xc * xc, axis=-1, keepdims=True)
    return xc * lax.rsqrt(var + LN_EPS) * g + b


def _add_ln_kernel(*refs, n_in, alpha, entry, out_dtypes):
    ins = refs[:n_in]
    params = refs[n_in:len(refs) - len(out_dtypes)]
    outs = refs[len(refs) - len(out_dtypes):]
    x = ins[0][...]
    if entry:
        x = _layer_norm(x, params[0][...], params[1][...])
    if alpha != 1.0:
        x = alpha * x
    for r in ins[1:]:
        x = x + r[...]
    y = _layer_norm(x, params[-2][...], params[-1][...])
    for o_ref in outs:
        o_ref[...] = y.astype(o_ref.dtype)


def _add_ln(ins, g, b, alpha, out_dtypes, entry=None, tm=256):
    m, d = ins[0].shape
    tm = _tile(m, tm)
    row = pl.BlockSpec((tm, d), lambda i: (i, 0))
    vec = pl.BlockSpec((1, d), lambda i: (0, 0))
    params = [t.reshape(1, d) for t in (tuple(entry) if entry else ()) + (g, b)]
    return pl.pallas_call(
        functools.partial(_add_ln_kernel, n_in=len(ins), alpha=alpha, entry=bool(entry),
                          out_dtypes=tuple(out_dtypes)),
        grid=(m // tm,),
        in_specs=[row] * len(ins) + [vec] * len(params),
        out_specs=[row] * len(out_dtypes),
        out_shape=[jax.ShapeDtypeStruct((m, d), dt) for dt in out_dtypes],
        compiler_params=_cparams("parallel"),
        name="add_ln",
    )(*ins, *params)


def _mm_kernel(a_ref, b_ref, o_ref):
    o_ref[...] = jnp.dot(a_ref[...], b_ref[...].astype(BF16), preferred_element_type=F32).astype(o_ref.dtype)


def _matmul(a, b, tm, tn, n=None, out_dtype=F32, name="matmul"):
    m, k = a.shape
    n = b.shape[1] if n is None else n
    tm, tn = _tile(m, tm), _tile(n, tn)
    return pl.pallas_call(
        _mm_kernel,
        grid=(m // tm, n // tn),
        in_specs=[pl.BlockSpec((tm, k), lambda i, j: (i, 0)),
                  pl.BlockSpec((k, tn), lambda i, j: (0, j))],
        out_specs=pl.BlockSpec((tm, tn), lambda i, j: (i, j)),
        out_shape=jax.ShapeDtypeStruct((m, n), out_dtype),
        compiler_params=_cparams("parallel", "arbitrary"),
        name=name,
    )(a, b)


def _mm_wt_kernel(a_ref, wt_ref, o_ref):
    o_ref[...] = lax.dot_general(a_ref[...], wt_ref[...].astype(BF16), NT_DIMS, preferred_element_type=F32)


def _matmul_wt(a, wt, row0, n, tm, tn, name):
    m, k = a.shape
    tm, tn = _tile(m, tm), _tile(n, tn)
    return pl.pallas_call(
        _mm_wt_kernel,
        grid=(m // tm, n // tn),
        in_specs=[pl.BlockSpec((tm, k), lambda i, j: (i, 0)),
                  pl.BlockSpec((pl.Element(tn), pl.Element(k)),
                               lambda i, j: (pl.multiple_of(row0 + j * tn, SUBLANES), 0))],
        out_specs=pl.BlockSpec((tm, tn), lambda i, j: (i, j)),
        out_shape=jax.ShapeDtypeStruct((m, n), F32),
        compiler_params=_cparams("parallel", "arbitrary"),
        name=name,
    )(a, wt)


def _mm_nt_kernel(w_ref, a_ref, o_ref):
    o_ref[...] = lax.dot_general(w_ref[...].astype(BF16), a_ref[...], NT_DIMS, preferred_element_type=F32)


def _matmul_nt(wt, row0, r, a, tm):
    k = wt.shape[1]
    m = a.shape[0]
    tm = _tile(m, tm)
    return pl.pallas_call(
        _mm_nt_kernel,
        grid=(m // tm,),
        in_specs=[pl.BlockSpec((pl.Element(r), pl.Element(k)), lambda i: (row0, 0)),
                  pl.BlockSpec((tm, k), lambda i: (i, 0))],
        out_specs=pl.BlockSpec((r, tm), lambda i: (0, i)),
        out_shape=jax.ShapeDtypeStruct((r, m), F32),
        compiler_params=_cparams("parallel"),
        name="gate_proj",
    )(wt, a)


def _merge_kernel(ha_ref, ob_ref, wa_ref, wb_ref, ma_ref, mb_ref, o_ref):
    ya = jnp.dot(ha_ref[...], wa_ref[...].astype(BF16), preferred_element_type=F32)
    yb = jnp.dot(ob_ref[...], wb_ref[...].astype(BF16), preferred_element_type=F32)
    o_ref[...] = (jax.nn.sigmoid(ma_ref[...]) * ya + jax.nn.sigmoid(mb_ref[...]) * yb).astype(o_ref.dtype)


def _merge(ha, ob, wa, wb, proj, ma_col, mb_col, tm=1024, tn=512):
    m, k = ha.shape
    n = wa.shape[1]
    tm, tn = _tile(m, tm), _tile(n, tn, ma_col, mb_col)
    ja, jb = ma_col // tn, mb_col // tn
    return pl.pallas_call(
        _merge_kernel,
        grid=(m // tm, n // tn),
        in_specs=[pl.BlockSpec((tm, k), lambda i, j: (i, 0)),
                  pl.BlockSpec((tm, k), lambda i, j: (i, 0)),
                  pl.BlockSpec((k, tn), lambda i, j: (0, j)),
                  pl.BlockSpec((k, tn), lambda i, j: (0, j)),
                  pl.BlockSpec((tm, tn), lambda i, j: (i, ja + j)),
                  pl.BlockSpec((tm, tn), lambda i, j: (i, jb + j))],
        out_specs=pl.BlockSpec((tm, tn), lambda i, j: (i, j)),
        out_shape=jax.ShapeDtypeStruct((m, n), BF16),
        compiler_params=_cparams("parallel", "arbitrary"),
        name="merge",
    )(ha, ob, wa, wb, proj, proj)


def _causal_conv(x, w):
    def taps(v, history):
        y = v * w[CONV_WIDTH - 1:CONV_WIDTH, :]
        for j in range(CONV_WIDTH - 1):
            y = y + history(v, CONV_WIDTH - 1 - j) * w[j:j + 1, :]
        return y

    head = x[:SUBLANES, :]
    head_row = lax.broadcasted_iota(I32, head.shape, 0)
    y_head = taps(head, lambda v, s: jnp.where(head_row >= s, pltpu.roll(v, s, 0), 0.0))
    y = taps(x, lambda v, s: pltpu.roll(v, s, 0))
    return jnp.concatenate([y_head, y[SUBLANES:, :]], axis=0)


def _rglru_kernel(x_ref, y_ref, cw_ref, cb_ref, wa_ref, ba_ref, wx_ref, bx_ref, lam_ref, o_ref):
    t, c = x_ref.shape
    row = lax.broadcasted_iota(I32, (t, c), 0)
    xr = _causal_conv(x_ref[...], cw_ref[...]) + cb_ref[...]
    xb = xr.astype(BF16)
    rs, is_ = [], []
    for k in range(c // RG_BLOCK):
        blk = xb[:, k * RG_BLOCK:(k + 1) * RG_BLOCK]
        rs.append(jnp.dot(blk, wa_ref[k], preferred_element_type=F32))
        is_.append(jnp.dot(blk, wx_ref[k], preferred_element_type=F32))
    r = jax.nn.sigmoid(jnp.concatenate(rs, axis=1) + ba_ref[...])
    i = jax.nn.sigmoid(jnp.concatenate(is_, axis=1) + bx_ref[...])
    log_a = (-RG_C) * r * jax.nn.softplus(-lam_ref[...])
    a = jnp.exp(log_a)
    b = jnp.sqrt(-jnp.tanh(log_a) * (a * a + 1.0)) * (i * xr)
    d = 1
    while d < t:
        keep = row >= d
        a_s = jnp.where(keep, pltpu.roll(a, d, 0), 1.0)
        b_s = jnp.where(keep, pltpu.roll(b, d, 0), 0.0)
        b = a * b_s + b
        a = a * a_s
        d *= 2
    o_ref[...] = (b * _gelu(y_ref[...])).astype(o_ref.dtype)


def _rglru(proj, seq, x_col, y_col, width, cw, cb, wa, ba, wx, bx, lam, ct=256):
    m = proj.shape[0]
    nb = m // seq
    ct = _tile(width, ct, x_col, y_col)
    jx, jy = x_col // ct, y_col // ct
    kb = ct // RG_BLOCK
    vec = pl.BlockSpec((1, ct), lambda b, c: (0, c))
    gate_w = pl.BlockSpec((kb, RG_BLOCK, RG_BLOCK), lambda b, c: (c, 0, 0))
    return pl.pallas_call(
        _rglru_kernel,
        grid=(nb, width // ct),
        in_specs=[pl.BlockSpec((seq, ct), lambda b, c: (b, jx + c)),
                  pl.BlockSpec((seq, ct), lambda b, c: (b, jy + c)),
                  pl.BlockSpec((CONV_WIDTH, ct), lambda b, c: (0, c)),
                  vec, gate_w, vec, gate_w, vec, vec],
        out_specs=pl.BlockSpec((seq, ct), lambda b, c: (b, c)),
        out_shape=jax.ShapeDtypeStruct((m, width), BF16),
        compiler_params=_cparams("parallel", "parallel"),
        name="rglru",
    )(proj, proj, cw, cb.reshape(1, width), wa.astype(BF16), ba.reshape(1, width),
      wx.astype(BF16), bx.reshape(1, width), lam.reshape(1, width))


def _dot_hi(a, b):
    return jnp.dot(a, b, precision=HIGHEST, preferred_element_type=F32)


def _split_bf16(x):
    hi = x.astype(BF16)
    return hi, x - hi.astype(F32)


def _lhs3(x_dup, first):
    _, lo = _split_bf16(x_dup)
    return jnp.concatenate([jnp.where(first, x_dup, lo).astype(BF16),
                            jnp.where(first, x_dup, 0.0).astype(BF16)], axis=1)


def _rhs3(p):
    hi, lo = _split_bf16(p)
    return jnp.concatenate([hi, hi, lo.astype(BF16), jnp.zeros_like(hi)], axis=0)


def _dot3(lhs3, rhs3):
    return jnp.dot(lhs3, rhs3, preferred_element_type=F32)


def _gdn_kernel(alog_ref, dtb_ref, q_ref, k_ref, v_ref, z_ref, cwq_ref, cwk_ref, cwv_ref, gt_ref, nw_ref,
                o_ref, qs, ks, vs, bts, *sets, heads_per_step, chunks_per_iter, ngroups, npairs):
    t, gw = q_ref.shape
    c = GDN_CHUNK
    dh = GDN_HEAD_DIM
    nchunks = t // c
    step = pl.program_id(0)
    hg = jnp.minimum(step, npairs - 1) % ngroups
    nheads = gt_ref.shape[0] // 2
    set_a, set_b = sets[:len(sets) // 2], sets[len(sets) // 2:]

    def prologue():
        q = _silu(_causal_conv(q_ref[...], cwq_ref[...]))
        k = _silu(_causal_conv(k_ref[...], cwk_ref[...]))
        vs[...] = _silu(_causal_conv(v_ref[...], cwv_ref[...]))
        for g in range(heads_per_step):
            sl = slice(g * dh, (g + 1) * dh)
            qh, kh = q[:, sl], k[:, sl]
            qs[:, sl] = qh * lax.rsqrt(jnp.sum(qh * qh, axis=-1, keepdims=True) + NORM_EPS) * (dh ** -0.5)
            ks[:, sl] = kh * lax.rsqrt(jnp.sum(kh * kh, axis=-1, keepdims=True) + NORM_EPS)

    ri = lax.broadcasted_iota(I32, (c, 2 * c), 0)
    lane = lax.broadcasted_iota(I32, (c, 2 * c), 1)
    ci = jnp.bitwise_and(lane, c - 1)
    first = lane < c
    first2 = lax.broadcasted_iota(I32, (2 * c, 2 * c), 1) < c
    eye_first = ri == lane
    causal = ri >= ci
    strict = ri > ci
    eye_f = (ri == ci).astype(F32)
    upper_f = (ri <= ci).astype(F32)

    def chunk_gates(gcs):
        for g in range(heads_per_step):
            h = hg * heads_per_step + g
            a_gate = gt_ref[nheads + h]
            b_gate = gt_ref[h]
            neg_rate = -jnp.exp(jnp.zeros_like(a_gate) + alog_ref[h])
            g_log = neg_rate * jax.nn.softplus(a_gate + dtb_ref[h])
            gcs[g] = _dot_hi(g_log, upper_f)
            bts[g] = _dot_hi(jax.nn.sigmoid(b_gate), eye_f)

    def to_col(row_vec):
        return jnp.sum(jnp.where(eye_first, jnp.broadcast_to(row_vec, (c, 2 * c)), 0.0), axis=1, keepdims=True)

    def prepare_step(i, dst, between):
        aus, qws, kws, kus, gcs = dst
        chains = [(i * chunks_per_iter + j, g) for j in range(chunks_per_iter) for g in range(heads_per_step)]
        rows = [pl.ds(pl.multiple_of(n * c, c), c) for n, _ in chains]
        cols = [slice(g * dh, (g + 1) * dh) for _, g in chains]
        every = range(len(chains))
        gc_row = [gcs[g, pl.ds(n, 1), :] for n, g in chains]
        gc_col = [to_col(r) for r in gc_row]
        beta_col = [to_col(bts[g, pl.ds(n, 1), :]) for n, g in chains]
        decay = [jnp.where(causal, jnp.exp(jnp.where(causal, gc_col[x] - gc_row[x], 0.0)), 0.0) for x in every]
        e_col = [jnp.exp(cv) for cv in gc_col]
        qc = [qs[rows[x], cols[x]] for x in every]
        kc = [ks[rows[x], cols[x]] for x in every]
        vc = [vs[rows[x], cols[x]] for x in every]
        kb = [kc[x] * beta_col[x] for x in every]

        kk = []
        k2_hi = []
        for x in every:
            kb_hi, kb_lo = _split_bf16(kb[x])
            hi, lo = _split_bf16(jnp.concatenate([kc[x], kc[x]], axis=0))
            k2_hi.append(hi)
            kk.append(lax.dot_general(jnp.concatenate([kb_hi, kb_lo.astype(BF16), kb_hi], axis=1),
                                      jnp.concatenate([hi, hi, lo.astype(BF16)], axis=1),
                                      NT_DIMS, preferred_element_type=F32))
        between()
        a_low = [jnp.where(strict, kk[x] * decay[x], 0.0) for x in every]
        inv = [eye_f - a for a in a_low]
        pw = [_dot3(_lhs3(a, first), _rhs3(a)) for a in a_low]
        between()
        span = 2
        while 2 * span < c:
            both = [_dot3(_lhs3(jnp.concatenate([inv[x], pw[x]], axis=0), first2), _rhs3(pw[x])) for x in every]
            inv = [inv[x] + both[x][:c] for x in every]
            pw = [b[c:] for b in both]
            span *= 2
            between()
        inv = [inv[x] + _dot3(_lhs3(inv[x], first), _rhs3(pw[x])) for x in every]
        between()
        sol = [_dot3(_lhs3(inv[x], first),
                     _rhs3(jnp.concatenate([vc[x] * beta_col[x], kb[x] * e_col[x]], axis=1))) for x in every]
        between()
        attn = [(lax.dot_general(qc[x].astype(BF16), k2_hi[x], NT_DIMS, preferred_element_type=F32)
                 * decay[x])[:, :c].astype(BF16) for x in every]
        u_b = [s[:, :dh].astype(BF16) for s in sol]
        w_b = [s[:, dh:].astype(BF16) for s in sol]
        kdt = [(kc[x] * jnp.exp(gc_row[x][:, c - 1:c] - gc_col[x])).T.astype(BF16) for x in every]
        attn_w = [jnp.dot(attn[x], w_b[x], preferred_element_type=F32) for x in every]
        attn_u = [jnp.dot(attn[x], u_b[x], preferred_element_type=F32) for x in every]
        kd_w = [jnp.dot(kdt[x], w_b[x], preferred_element_type=F32) for x in every]
        kd_u = [jnp.dot(kdt[x], u_b[x], preferred_element_type=F32) for x in every]
        for x, (n, g) in enumerate(chains):
            qws[rows[x], cols[x]] = (qc[x] * e_col[x] - attn_w[x]).astype(BF16)
            aus[rows[x], cols[x]] = attn_u[x]
            kws[g, n] = kd_w[x].astype(BF16)
            kus[g, n] = kd_u[x]

    def state_step(n, state, src):
        aus, qws, kws, kus, gcs = src
        rows = pl.ds(pl.multiple_of(n * c, c), c)
        heads = range(heads_per_step)
        cols = [slice(g * dh, (g + 1) * dh) for g in heads]
        state_b = [s.astype(BF16) for s in state]
        decay_last = [jnp.exp(gcs[g, pl.ds(n, 1), :][:, c - 1:c]) for g in heads]
        new_state = tuple(state[g] * decay_last[g] + kus[g, n]
                          - jnp.dot(kws[g, n], state_b[g], preferred_element_type=F32) for g in heads)
        for g in heads:
            o_c = aus[rows, cols[g]] + jnp.dot(qws[rows, cols[g]], state_b[g], preferred_element_type=F32)
            o_n = o_c * lax.rsqrt(jnp.mean(o_c * o_c, axis=-1, keepdims=True) + NORM_EPS)
            o_ref[rows, cols[g]] = (o_n * nw_ref[...] * _silu(z_ref[rows, cols[g]])).astype(o_ref.dtype)
        return new_state

    zero_state = tuple(jnp.zeros((dh, dh), F32) for _ in range(heads_per_step))
    n_iters = nchunks // chunks_per_iter

    def run(dst, src):
        if dst is None:
            lax.fori_loop(0, nchunks, lambda n, state: state_step(n, state, src), zero_state)
            return
        prologue()
        chunk_gates(dst[-1])
        if src is None:
            def only_prepare(i, carry):
                prepare_step(i, dst, lambda: None)
                return carry
            lax.fori_loop(0, n_iters, only_prepare, 0)
            return

        def both(i, state):
            progress = {"state": state, "done": 0}

            def between():
                if progress["done"] < chunks_per_iter:
                    progress["state"] = state_step(i * chunks_per_iter + progress["done"], progress["state"], src)
                    progress["done"] += 1

            prepare_step(i, dst, between)
            while progress["done"] < chunks_per_iter:
                between()
            return progress["state"]

        lax.fori_loop(0, n_iters, both, zero_state)

    inner = jnp.logical_and(step > 0, step < npairs)
    even = step % 2 == 0
    pl.when(step == 0)(functools.partial(run, set_a, None))
    pl.when(jnp.logical_and(inner, even))(functools.partial(run, set_a, set_b))
    pl.when(jnp.logical_and(inner, jnp.logical_not(even)))(functools.partial(run, set_b, set_a))
    pl.when(step == npairs)(functools.partial(run, None, set_a if (npairs - 1) % 2 == 0 else set_b))


def _gdn(proj, gates_t, seq, q_col, k_col, v_col, z_col, width, conv_w, a_log, dt_bias, norm_w,
         heads_per_step=2, chunks_per_iter=8):
    m = proj.shape[0]
    nb = m // seq
    gw = heads_per_step * GDN_HEAD_DIM
    nh = width // GDN_HEAD_DIM
    nchunks = seq // GDN_CHUNK
    chunks_per_iter = math.gcd(chunks_per_iter, nchunks)
    jq, jk, jv, jz = (col // gw for col in (q_col, k_col, v_col, z_col))
    wblocks = width // gw
    gt3 = gates_t.reshape(2 * nh, m // GDN_CHUNK, GDN_CHUNK)
    smem = pl.BlockSpec(memory_space=pltpu.SMEM)
    ngroups = nh // heads_per_step
    npairs = nb * ngroups

    def pair(s):
        sc = jnp.minimum(s, npairs - 1)
        return sc // ngroups, sc % ngroups

    def prev_pair(s):
        sp = jnp.maximum(s - 1, 0)
        return sp // ngroups, sp % ngroups

    def col_spec(j0, which=pair):
        return pl.BlockSpec((seq, gw), lambda s: (which(s)[0], j0 + which(s)[1]))

    def cw_spec(j0):
        return pl.BlockSpec((CONV_WIDTH, gw), lambda s: (0, j0 + pair(s)[1]))

    prepared = [pltpu.VMEM((seq, gw), F32), pltpu.VMEM((seq, gw), BF16),
                pltpu.VMEM((heads_per_step, nchunks, GDN_HEAD_DIM, GDN_HEAD_DIM), BF16),
                pltpu.VMEM((heads_per_step, nchunks, GDN_HEAD_DIM, GDN_HEAD_DIM), F32),
                pltpu.VMEM((heads_per_step, nchunks, 2 * GDN_CHUNK), F32)]
    return pl.pallas_call(
        functools.partial(_gdn_kernel, heads_per_step=heads_per_step, chunks_per_iter=chunks_per_iter,
                          ngroups=ngroups, npairs=npairs),
        grid=(npairs + 1,),
        in_specs=[smem, smem, col_spec(jq), col_spec(jk), col_spec(jv), col_spec(jz, prev_pair),
                  cw_spec(0), cw_spec(wblocks), cw_spec(2 * wblocks),
                  pl.BlockSpec((2 * nh, nchunks, GDN_CHUNK), lambda s: (0, pair(s)[0], 0)),
                  pl.BlockSpec((1, GDN_HEAD_DIM), lambda s: (0, 0))],
        out_specs=pl.BlockSpec((seq, gw), lambda s: prev_pair(s)),
        out_shape=jax.ShapeDtypeStruct((m, width), BF16),
        scratch_shapes=[pltpu.VMEM((seq, gw), F32), pltpu.VMEM((seq, gw), F32), pltpu.VMEM((seq, gw), F32),
                        pltpu.VMEM((heads_per_step, nchunks, 2 * GDN_CHUNK), F32)] + prepared + prepared,
        compiler_params=_cparams("arbitrary"),
        name="gdn",
    )(a_log, dt_bias, proj, proj, proj, proj, conv_w, conv_w, conv_w, gt3, norm_w.reshape(1, GDN_HEAD_DIM))


def _top_k_rows(s, k, payload=None):
    r = s.shape[0]
    rowi = lax.broadcasted_iota(I32, s.shape, 0)
    vals, picks = [], []
    for _ in range(k):
        mx = jnp.max(s, axis=0, keepdims=True)
        arg = jnp.min(jnp.where(s == mx, rowi, r), axis=0, keepdims=True)
        hit = rowi == arg
        vals.append(mx)
        if payload is None:
            picks.append(arg)
        else:
            picks.append(jnp.max(jnp.where(hit, payload, -1), axis=0, keepdims=True))
        s = jnp.where(hit, -jnp.inf, s)
    return jnp.concatenate(vals, axis=0), jnp.concatenate(picks, axis=0)


def _select_experts(s1_t, s2_t):
    kk = PEER_TOPK
    s1, i1 = _top_k_rows(s1_t, kk)
    s2, i2 = _top_k_rows(s2_t, kk)
    cand_parts, id_parts = [], []
    a = 0
    while kk // (a + 1) > 1:
        nb = kk // (a + 1)
        nbp = -(-nb // SUBLANES) * SUBLANES
        part = s1[a:a + 1, :] + s2[:nbp, :]
        if nbp > nb:
            part = jnp.where(lax.broadcasted_iota(I32, part.shape, 0) < nb, part, -jnp.inf)
        cand_parts.append(part)
        id_parts.append(i1[a:a + 1, :] * PEER_NKEYS + i2[:nbp, :])
        a += 1
    cand_parts.append(s1[a:, :] + s2[0:1, :])
    id_parts.append(i1[a:, :] * PEER_NKEYS + i2[0:1, :])
    best, ids = _top_k_rows(jnp.concatenate(cand_parts, axis=0), kk,
                            payload=jnp.concatenate(id_parts, axis=0))
    e = jnp.exp(best - best[0:1, :])
    gates = e / jnp.sum(e, axis=0, keepdims=True)
    return jnp.right_shift(ids, PEER_NKEYS.bit_length() - 1), jnp.bitwise_and(ids, PEER_NKEYS - 1), gates


def _ple_route_kernel(x_ref, wg_ref, bg_ref, p_ref, wp_ref, q_ref, keys_ref, o_ref, i1_ref, i2_ref, gt_ref):
    k = x_ref.shape[1]
    kc = k // PEER_HEADS
    rt = q_ref.shape[0]
    acc = None
    for h in range(PEER_HEADS):
        ks = slice(h * kc, (h + 1) * kc)
        part = jnp.dot(x_ref[:, ks], wg_ref[ks, :].astype(BF16), preferred_element_type=F32)
        acc = part if acc is None else acc + part
        rows = slice(h * PEER_TOPK, (h + 1) * PEER_TOPK)
        for lg in range(rt // LANES):
            tok = slice(lg * LANES, (lg + 1) * LANES)
            scores = []
            for p in range(2):
                c0 = (h * 2 + p) * PEER_HALF
                scores.append(lax.dot_general(keys_ref[h, p].astype(BF16),
                                              q_ref[tok, c0:c0 + PEER_HALF].astype(BF16),
                                              NT_DIMS, preferred_element_type=F32))
            i1, i2, gates = _select_experts(*scores)
            i1_ref[rows, tok] = i1
            i2_ref[rows, tok] = i2
            gt_ref[rows, tok] = gates
    gate = jax.nn.sigmoid(acc + bg_ref[...])
    o_ref[...] = gate * jnp.dot(p_ref[...].astype(BF16), wp_ref[...].astype(BF16), preferred_element_type=F32)


def _ple_route(x1b, wg, bg, p, wp, q, keys, tm=1024, tn=512):
    m, k = x1b.shape
    n = wg.shape[1]
    kp = p.shape[1]
    qd = q.shape[1]
    tm, tn = _tile(m, tm), _tile(n, tn)
    nj = n // tn
    rt = m // ((m // tm) * nj)
    assert rt % LANES == 0 and k % PEER_HEADS == 0, (rt, k)
    nslots = PEER_HEADS * PEER_TOPK
    table = pl.BlockSpec((nslots, rt), lambda i, j: (0, i * nj + j))
    return pl.pallas_call(
        _ple_route_kernel,
        grid=(m // tm, nj),
        in_specs=[pl.BlockSpec((tm, k), lambda i, j: (i, 0)),
                  pl.BlockSpec((k, tn), lambda i, j: (0, j)),
                  pl.BlockSpec((1, tn), lambda i, j: (0, j)),
                  pl.BlockSpec((tm, kp), lambda i, j: (i, 0)),
                  pl.BlockSpec((kp, tn), lambda i, j: (0, j)),
                  pl.BlockSpec((rt, qd), lambda i, j: (i * nj + j, 0)),
                  pl.BlockSpec(keys.shape, lambda i, j: (0, 0, 0, 0))],
        out_specs=[pl.BlockSpec((tm, tn), lambda i, j: (i, j)), table, table, table],
        out_shape=[jax.ShapeDtypeStruct((m, n), F32), jax.ShapeDtypeStruct((nslots, m), I32),
                   jax.ShapeDtypeStruct((nslots, m), I32), jax.ShapeDtypeStruct((nslots, m), F32)],
        compiler_params=_cparams("parallel", "arbitrary"),
        name="ple_route",
    )(x1b, wg, bg.reshape(1, n), p, wp, q, keys)


def _gates_kernel(i1t_ref, i2t_ref, gtt_ref, u_ref, o_ref, ub_ref, i1s, i2s, gts):
    tm = i1t_ref.shape[1]
    kk = PEER_TOPK
    ub_ref[...] = u_ref[...].astype(BF16)
    i1s[...] = i1t_ref[...].T
    i2s[...] = i2t_ref[...].T
    gts[...] = gtt_ref[...].T

    nslots = PEER_HEADS * kk
    key_iota = lax.broadcasted_iota(I32, (PEER_NKEYS, nslots), 0)

    def group(gi):
        base = pl.multiple_of(gi * GATE_GROUP, GATE_GROUP)
        tiles = []
        for tk in range(GATE_GROUP):
            i1r = jnp.broadcast_to(i1s[pl.ds(base + tk, 1), :], (PEER_NKEYS, nslots))
            i2r = jnp.broadcast_to(i2s[pl.ds(base + tk, 1), :], (PEER_NKEYS, nslots))
            gr = jnp.broadcast_to(gts[pl.ds(base + tk, 1), :], (PEER_NKEYS, nslots))
            a_t = jnp.where(key_iota == i1r, gr, 0.0).astype(BF16)
            b_t = jnp.where(key_iota == i2r, 1.0, 0.0).astype(BF16)
            tiles.append(lax.dot_general(a_t, b_t, NT_DIMS, preferred_element_type=F32))
        o_ref[gi] = jnp.swapaxes(jnp.stack(tiles, axis=0), 0, 1).astype(o_ref.dtype)

    def groups(it, carry):
        for k in range(ROUTE_GROUPS_PER_ITER):
            group(it * ROUTE_GROUPS_PER_ITER + k)
        return carry

    lax.fori_loop(0, tm // (GATE_GROUP * ROUTE_GROUPS_PER_ITER), groups, 0)


def _gates(i1t, i2t, gtt, u_tab, tm=128):
    nslots, m = i1t.shape
    tm = _tile(m, tm)
    nsteps = m // tm
    ne, d = u_tab.shape
    er = ne // nsteps
    assert er * nsteps == ne and er % (2 * SUBLANES) == 0, (ne, nsteps)
    table = pl.BlockSpec((er, d), lambda i: (i, 0))
    slots = pl.BlockSpec((nslots, tm), lambda i: (0, i))
    return pl.pallas_call(
        _gates_kernel,
        grid=(nsteps,),
        in_specs=[slots, slots, slots, table],
        out_specs=[pl.BlockSpec((tm // GATE_GROUP, PEER_NKEYS, GATE_GROUP, PEER_NKEYS), lambda i: (i, 0, 0, 0)),
                   table],
        out_shape=[jax.ShapeDtypeStruct((m // GATE_GROUP, PEER_NKEYS, GATE_GROUP, PEER_NKEYS), BF16),
                   jax.ShapeDtypeStruct((ne, d), BF16)],
        scratch_shapes=[pltpu.VMEM((tm, nslots), I32), pltpu.VMEM((tm, nslots), I32),
                        pltpu.VMEM((tm, nslots), F32)],
        compiler_params=_cparams("parallel"),
        name="peer_gates",
    )(i1t, i2t, gtt, u_tab)


def _peer_kernel(x_ref, u_ref, v_ref, g_ref, o_ref, w_scr, *, i1_per_step, nblk):
    s = pl.program_id(0)
    last = pl.num_programs(0) - 1
    tm = x_ref.shape[0]
    opens_tile = (s - 1) % nblk == 0

    def hidden():
        hid = lax.dot_general(x_ref[...], u_ref[...], NT_DIMS, preferred_element_type=F32)
        parts = []
        for gi in range(i1_per_step):
            gate = g_ref[:, gi].reshape(tm, PEER_NKEYS).astype(F32)
            parts.append((gate * _gelu(hid[:, gi * PEER_NKEYS:(gi + 1) * PEER_NKEYS])).astype(BF16))
        return jnp.concatenate(parts, axis=1)

    @pl.when(s == 0)
    def _():
        w_scr[...] = hidden()

    def both(first_of_tile):
        w_prev = w_scr[...]
        w_scr[...] = hidden()
        contrib = jnp.dot(w_prev, v_ref[...].astype(BF16), preferred_element_type=F32)
        if first_of_tile:
            o_ref[...] = contrib
        else:
            o_ref[...] += contrib

    inner = jnp.logical_and(s > 0, s < last)
    pl.when(jnp.logical_and(inner, opens_tile))(functools.partial(both, True))
    pl.when(jnp.logical_and(inner, jnp.logical_not(opens_tile)))(functools.partial(both, False))

    @pl.when(s == last)
    def _():
        o_ref[...] += jnp.dot(w_scr[...], v_ref[...].astype(BF16), preferred_element_type=F32)


def _peer(x1b, ub, vb, gates, tm=512, i1_per_step=4):
    m, d = x1b.shape
    ne = ub.shape[0]
    tm = _tile(m, tm)
    te = i1_per_step * PEER_NKEYS
    nblk = ne // te
    assert nblk > 1, "the last step accumulates: a token tile needs more than one expert block"
    npairs = (m // tm) * nblk

    def pair(s):
        sc = jnp.minimum(s, npairs - 1)
        return sc // nblk, sc % nblk

    def prev_pair(s):
        sp = jnp.maximum(s - 1, 0)
        return sp // nblk, sp % nblk

    return pl.pallas_call(
        functools.partial(_peer_kernel, i1_per_step=i1_per_step, nblk=nblk),
        grid=(npairs + 1,),
        in_specs=[pl.BlockSpec((tm, d), lambda s: (pair(s)[0], 0)),
                  pl.BlockSpec((te, d), lambda s: (pair(s)[1], 0)),
                  pl.BlockSpec((te, d), lambda s: (prev_pair(s)[1], 0)),
                  pl.BlockSpec((tm // GATE_GROUP, i1_per_step, GATE_GROUP, PEER_NKEYS),
                               lambda s: (pair(s)[0], pair(s)[1], 0, 0))],
        out_specs=pl.BlockSpec((tm, d), lambda s: (prev_pair(s)[0], 0)),
        out_shape=jax.ShapeDtypeStruct((m, d), F32),
        scratch_shapes=[pltpu.VMEM((tm, te), BF16)],
        compiler_params=_cparams("arbitrary"),
        name="peer_experts",
    )(x1b, ub, vb, gates)


def _layer(h, hb, p, seq, w_in, rg_conv_w, rg_conv_b, rg_wa, rg_ba, rg_wx, rg_bx, rg_lambda, rg_out,
           gdn_conv_w, gdn_a_log, gdn_dt_bias, gdn_norm_w, gdn_out, w_o, ln1_g, ln1_b, peer_wq, peer_keys,
           peer_u, peer_v, ple_w, ple_gate_w, ple_gate_b, ln2_g, ln2_b, alpha, last, entry):
    d = h.shape[1]
    rg_w = rg_out.shape[0]
    gdn_w = gdn_out.shape[0]
    nh = gdn_w // GDN_HEAD_DIM
    n_main = 2 * rg_w + 4 * gdn_w
    w_in_t = w_in.T
    cols = {"rg_x": 0, "rg_y": rg_w, "q": 2 * rg_w, "k": 2 * rg_w + gdn_w, "v": 2 * rg_w + 2 * gdn_w,
            "z": 2 * rg_w + 3 * gdn_w}

    proj = _matmul_wt(hb, w_in_t, 0, n_main, tm=1024, tn=512, name="in_proj")
    proj_m = _matmul_wt(hb, w_in_t, n_main + 2 * nh, 2 * d, tm=1024, tn=512, name="in_proj_merge")
    gates_t = _matmul_nt(w_in_t, n_main, 2 * nh, hb, tm=512)

    h_a = _rglru(proj, seq, cols["rg_x"], cols["rg_y"], rg_w, rg_conv_w, rg_conv_b, rg_wa, rg_ba,
                 rg_wx, rg_bx, rg_lambda)
    o_b = _gdn(proj, gates_t, seq, cols["q"], cols["k"], cols["v"], cols["z"], gdn_w, gdn_conv_w,
               gdn_a_log, gdn_dt_bias, gdn_norm_w)
    merged = _merge(h_a, o_b, rg_out, gdn_out, proj_m, 0, d)
    mix = _matmul(merged, w_o, tm=1024, tn=512, name="out_proj")
    x1, x1b = _add_ln([h, mix], ln1_g, ln1_b, alpha, (F32, BF16), entry=entry)

    q = _matmul(x1b, peer_wq, tm=1024, tn=512, name="peer_query")
    ple, i1t, i2t, gtt = _ple_route(x1b, ple_gate_w, ple_gate_b, p, ple_w, q, peer_keys)
    gates, peer_ub = _gates(i1t, i2t, gtt, peer_u)
    ffn = _peer(x1b, peer_ub, peer_v, gates)
    out = _add_ln([x1, ffn, ple], ln2_g, ln2_b, alpha, (F32,) if last else (F32, BF16))
    return (out[0], None) if last else tuple(out)


def kernel(x, p, ln_emb_g, ln_emb_b, w_in, rg_conv_w, rg_conv_b, rg_wa, rg_ba, rg_wx, rg_bx, rg_lambda, rg_out, gdn_conv_w, gdn_a_log, gdn_dt_bias, gdn_norm_w, gdn_out, w_o, ln1_g, ln1_b, peer_wq, peer_keys, peer_u, peer_v, ple_w, ple_gate_w, ple_gate_b, ln2_g, ln2_b):
    bsz, seq, d = x.shape
    depth = w_in.shape[0]
    m = bsz * seq
    alpha = (2.0 * depth) ** 0.25
    h = x.reshape(m, d)
    entry = (ln_emb_g, ln_emb_b)
    (hb,) = _add_ln([h], ln_emb_g, ln_emb_b, 1.0, (BF16,))
    for i in range(depth):
        h, hb = _layer(h, hb, p[i].reshape(m, -1), seq, w_in[i], rg_conv_w[i], rg_conv_b[i], rg_wa[i], rg_ba[i],
                       rg_wx[i], rg_bx[i], rg_lambda[i], rg_out[i], gdn_conv_w[i], gdn_a_log[i], gdn_dt_bias[i],
                       gdn_norm_w[i], gdn_out[i], w_o[i], ln1_g[i], ln1_b[i], peer_wq[i], peer_keys[i],
                       peer_u[i], peer_v[i], ple_w[i], ple_gate_w[i], ple_gate_b[i], ln2_g[i], ln2_b[i], alpha,
                       i == depth - 1, entry)
        entry = None
    return h.reshape(bsz, seq, d)
```

```python
import functools
import math

import jax
import jax.numpy as jnp
from jax import lax
from jax.experimental import pallas as pl
from jax.experimental.pallas import tpu as pltpu

F32 = jnp.float32
BF16 = jnp.bfloat16
I32 = jnp.int32

LANES = 128
SUBLANES = 8
VMEM_LIMIT = 58 * 1024 * 1024

CONV_WIDTH = 4
RG_BLOCK = 128
RG_C = 8.0
GDN_HEAD_DIM = 128
GDN_CHUNK = 64
PEER_HEADS = 8
PEER_NKEYS = 128
PEER_HALF = 128
PEER_TOPK = 16
GATE_GROUP = 16
ROUTE_GROUPS_PER_ITER = 2
LN_EPS = 1e-5
NORM_EPS = 1e-6
HIGHEST = lax.Precision.HIGHEST
NT_DIMS = (((1,), (1,)), ((), ()))


def _cparams(*sem):
    return pltpu.CompilerParams(dimension_semantics=sem, vmem_limit_bytes=VMEM_LIMIT)


def _tile(n, target, *also_divides):
    return math.gcd(target, n, *also_divides)


def _gelu(x):
    return 0.5 * x * (1.0 + lax.erf(x * (2.0 ** -0.5)))


def _silu(x):
    return x * jax.nn.sigmoid(x)


def _layer_norm(x, g, b):
    mu = jnp.mean(x, axis=-1, keepdims=True)
    xc = x - mu
    var = jnp.mean(xc * xc, axis=-1, keepdims=True)
    return xc * lax.rsqrt(var + LN_EPS) * g + b


def _add_ln_kernel(*refs, n_in, alpha, entry, out_dtypes):
    ins = refs[:n_in]
    params = refs[n_in:len(refs) - len(out_dtypes)]
    outs = refs[len(refs) - len(out_dtypes):]
    x = ins[0][...]
    if entry:
        x = _layer_norm(x, params[0][...], params[1][...])
    if alpha != 1.0:
        x = alpha * x
    for r in ins[1:]:
        x = x + r[...]
    y = _layer_norm(x, params[-2][...], params[-1][...])
    for o_ref in outs:
        o_ref[...] = y.astype(o_ref.dtype)


def _add_ln(ins, g, b, alpha, out_dtypes, entry=None, tm=256):
    m, d = ins[0].shape
    tm = _tile(m, tm)
    row = pl.BlockSpec((tm, d), lambda i: (i, 0))
    vec = pl.BlockSpec((1, d), lambda i: (0, 0))
    params = [t.reshape(1, d) for t in (tuple(entry) if entry else ()) + (g, b)]
    return pl.pallas_call(
        functools.partial(_add_ln_kernel, n_in=len(ins), alpha=alpha, entry=bool(entry),
                          out_dtypes=tuple(out_dtypes)),
        grid=(m // tm,),
        in_specs=[row] * len(ins) + [vec] * len(params),
        out_specs=[row] * len(out_dtypes),
        out_shape=[jax.ShapeDtypeStruct((m, d), dt) for dt in out_dtypes],
        compiler_params=_cparams("parallel"),
        name="add_ln",
    )(*ins, *params)


def _mm_kernel(a_ref, b_ref, o_ref):
    o_ref[...] = jnp.dot(a_ref[...], b_ref[...].astype(BF16), preferred_element_type=F32).astype(o_ref.dtype)


def _matmul(a, b, tm, tn, n=None, out_dtype=F32, name="matmul"):
    m, k = a.shape
    n = b.shape[1] if n is None else n
    tm, tn = _tile(m, tm), _tile(n, tn)
    return pl.pallas_call(
        _mm_kernel,
        grid=(m // tm, n // tn),
        in_specs=[pl.BlockSpec((tm, k), lambda i, j: (i, 0)),
                  pl.BlockSpec((k, tn), lambda i, j: (0, j))],
        out_specs=pl.BlockSpec((tm, tn), lambda i, j: (i, j)),
        out_shape=jax.ShapeDtypeStruct((m, n), out_dtype),
        compiler_params=_cparams("parallel", "arbitrary"),
        name=name,
    )(a, b)


def _mm_wt_kernel(a_ref, wt_ref, o_ref):
    o_ref[...] = lax.dot_general(a_ref[...], wt_ref[...].astype(BF16), NT_DIMS, preferred_element_type=F32)


def _matmul_wt(a, wt, row0, n, tm, tn, name):
    m, k = a.shape
    tm, tn = _tile(m, tm), _tile(n, tn)
    return pl.pallas_call(
        _mm_wt_kernel,
        grid=(m // tm, n // tn),
        in_specs=[pl.BlockSpec((tm, k), lambda i, j: (i, 0)),
                  pl.BlockSpec((pl.Element(tn), pl.Element(k)),
                               lambda i, j: (pl.multiple_of(row0 + j * tn, SUBLANES), 0))],
        out_specs=pl.BlockSpec((tm, tn), lambda i, j: (i, j)),
        out_shape=jax.ShapeDtypeStruct((m, n), F32),
        compiler_params=_cparams("parallel", "arbitrary"),
        name=name,
    )(a, wt)


def _mm_nt_kernel(w_ref, a_ref, o_ref):
    o_ref[...] = lax.dot_general(w_ref[...].astype(BF16), a_ref[...], NT_DIMS, preferred_element_type=F32)


def _matmul_nt(wt, row0, r, a, tm):
    k = wt.shape[1]
    m = a.shape[0]
    tm = _tile(m, tm)
    return pl.pallas_call(
        _mm_nt_kernel,
        grid=(m // tm,),
        in_specs=[pl.BlockSpec((pl.Element(r), pl.Element(k)), lambda i: (row0, 0)),
                  pl.BlockSpec((tm, k), lambda i: (i, 0))],
        out_specs=pl.BlockSpec((r, tm), lambda i: (0, i)),
        out_shape=jax.ShapeDtypeStruct((r, m), F32),
        compiler_params=_cparams("parallel"),
        name="gate_proj",
    )(wt, a)


def _merge_kernel(ha_ref, ob_ref, wa_ref, wb_ref, ma_ref, mb_ref, o_ref):
    ya = jnp.dot(ha_ref[...], wa_ref[...].astype(BF16), preferred_element_type=F32)
    yb = jnp.dot(ob_ref[...], wb_ref[...].astype(BF16), preferred_element_type=F32)
    o_ref[...] = (jax.nn.sigmoid(ma_ref[...]) * ya + jax.nn.sigmoid(mb_ref[...]) * yb).astype(o_ref.dtype)


def _merge(ha, ob, wa, wb, proj, ma_col, mb_col, tm=1024, tn=512):
    m, k = ha.shape
    n = wa.shape[1]
    tm, tn = _tile(m, tm), _tile(n, tn, ma_col, mb_col)
    ja, jb = ma_col // tn, mb_col // tn
    return pl.pallas_call(
        _merge_kernel,
        grid=(m // tm, n // tn),
        in_specs=[pl.BlockSpec((tm, k), lambda i, j: (i, 0)),
                  pl.BlockSpec((tm, k), lambda i, j: (i, 0)),
                  pl.BlockSpec((k, tn), lambda i, j: (0, j)),
                  pl.BlockSpec((k, tn), lambda i, j: (0, j)),
                  pl.BlockSpec((tm, tn), lambda i, j: (i, ja + j)),
                  pl.BlockSpec((tm, tn), lambda i, j: (i, jb + j))],
        out_specs=pl.BlockSpec((tm, tn), lambda i, j: (i, j)),
        out_shape=jax.ShapeDtypeStruct((m, n), BF16),
        compiler_params=_cparams("parallel", "arbitrary"),
        name="merge",
    )(ha, ob, wa, wb, proj, proj)


def _causal_conv(x, w):
    def taps(v, history):
        y = v * w[CONV_WIDTH - 1:CONV_WIDTH, :]
        for j in range(CONV_WIDTH - 1):
            y = y + history(v, CONV_WIDTH - 1 - j) * w[j:j + 1, :]
        return y

    head = x[:SUBLANES, :]
    head_row = lax.broadcasted_iota(I32, head.shape, 0)
    y_head = taps(head, lambda v, s: jnp.where(head_row >= s, pltpu.roll(v, s, 0), 0.0))
    y = taps(x, lambda v, s: pltpu.roll(v, s, 0))
    return jnp.concatenate([y_head, y[SUBLANES:, :]], axis=0)


def _rglru_kernel(x_ref, y_ref, cw_ref, cb_ref, wa_ref, ba_ref, wx_ref, bx_ref, lam_ref, o_ref):
    t, c = x_ref.shape
    row = lax.broadcasted_iota(I32, (t, c), 0)
    xr = _causal_conv(x_ref[...], cw_ref[...]) + cb_ref[...]
    xb = xr.astype(BF16)
    rs, is_ = [], []
    for k in range(c // RG_BLOCK):
        blk = xb[:, k * RG_BLOCK:(k + 1) * RG_BLOCK]
        rs.append(jnp.dot(blk, wa_ref[k], preferred_element_type=F32))
        is_.append(jnp.dot(blk, wx_ref[k], preferred_element_type=F32))
    r = jax.nn.sigmoid(jnp.concatenate(rs, axis=1) + ba_ref[...])
    i = jax.nn.sigmoid(jnp.concatenate(is_, axis=1) + bx_ref[...])
    log_a = (-RG_C) * r * jax.nn.softplus(-lam_ref[...])
    a = jnp.exp(log_a)
    b = jnp.sqrt(-jnp.tanh(log_a) * (a * a + 1.0)) * (i * xr)
    d = 1
    while d < t:
        keep = row >= d
        a_s = jnp.where(keep, pltpu.roll(a, d, 0), 1.0)
        b_s = jnp.where(keep, pltpu.roll(b, d, 0), 0.0)
        b = a * b_s + b
        a = a * a_s
        d *= 2
    o_ref[...] = (b * _gelu(y_ref[...])).astype(o_ref.dtype)


def _rglru(proj, seq, x_col, y_col, width, cw, cb, wa, ba, wx, bx, lam, ct=256):
    m = proj.shape[0]
    nb = m // seq
    ct = _tile(width, ct, x_col, y_col)
    jx, jy = x_col // ct, y_col // ct
    kb = ct // RG_BLOCK
    vec = pl.BlockSpec((1, ct), lambda b, c: (0, c))
    gate_w = pl.BlockSpec((kb, RG_BLOCK, RG_BLOCK), lambda b, c: (c, 0, 0))
    return pl.pallas_call(
        _rglru_kernel,
        grid=(nb, width // ct),
        in_specs=[pl.BlockSpec((seq, ct), lambda b, c: (b, jx + c)),
                  pl.BlockSpec((seq, ct), lambda b, c: (b, jy + c)),
                  pl.BlockSpec((CONV_WIDTH, ct), lambda b, c: (0, c)),
                  vec, gate_w, vec, gate_w, vec, vec],
        out_specs=pl.BlockSpec((seq, ct), lambda b, c: (b, c)),
        out_shape=jax.ShapeDtypeStruct((m, width), BF16),
        compiler_params=_cparams("parallel", "parallel"),
        name="rglru",
    )(proj, proj, cw, cb.reshape(1, width), wa.astype(BF16), ba.reshape(1, width),
      wx.astype(BF16), bx.reshape(1, width), lam.reshape(1, width))


def _dot_hi(a, b):
    return jnp.dot(a, b, precision=HIGHEST, preferred_element_type=F32)


def _split_bf16(x):
    hi = x.astype(BF16)
    return hi, x - hi.astype(F32)


def _lhs3(x_dup, first):
    _, lo = _split_bf16(x_dup)
    return jnp.concatenate([jnp.where(first, x_dup, lo).astype(BF16),
                            jnp.where(first, x_dup, 0.0).astype(BF16)], axis=1)


def _rhs3(p):
    hi, lo = _split_bf16(p)
    return jnp.concatenate([hi, hi, lo.astype(BF16), jnp.zeros_like(hi)], axis=0)


def _dot3(lhs3, rhs3):
    return jnp.dot(lhs3, rhs3, preferred_element_type=F32)


def _gdn_kernel(alog_ref, dtb_ref, q_ref, k_ref, v_ref, z_ref, cwq_ref, cwk_ref, cwv_ref, gt_ref, nw_ref,
                o_ref, qs, ks, vs, bts, *sets, heads_per_step, chunks_per_iter, ngroups, npairs):
    t, gw = q_ref.shape
    c = GDN_CHUNK
    dh = GDN_HEAD_DIM
    nchunks = t // c
    step = pl.program_id(0)
    hg = jnp.minimum(step, npairs - 1) % ngroups
    nheads = gt_ref.shape[0] // 2
    set_a, set_b = sets[:len(sets) // 2], sets[len(sets) // 2:]

    def prologue():
        q = _silu(_causal_conv(q_ref[...], cwq_ref[...]))
        k = _silu(_causal_conv(k_ref[...], cwk_ref[...]))
        vs[...] = _silu(_causal_conv(v_ref[...], cwv_ref[...]))
        for g in range(heads_per_step):
            sl = slice(g * dh, (g + 1) * dh)
            qh, kh = q[:, sl], k[:, sl]
            qs[:, sl] = qh * lax.rsqrt(jnp.sum(qh * qh, axis=-1, keepdims=True) + NORM_EPS) * (dh ** -0.5)
            ks[:, sl] = kh * lax.rsqrt(jnp.sum(kh * kh, axis=-1, keepdims=True) + NORM_EPS)

    ri = lax.broadcasted_iota(I32, (c, 2 * c), 0)
    lane = lax.broadcasted_iota(I32, (c, 2 * c), 1)
    ci = jnp.bitwise_and(lane, c - 1)
    first = lane < c
    first2 = lax.broadcasted_iota(I32, (2 * c, 2 * c), 1) < c
    eye_first = ri == lane
    causal = ri >= ci
    strict = ri > ci
    eye_f = (ri == ci).astype(F32)
    upper_f = (ri <= ci).astype(F32)

    def chunk_gates(gcs):
        for g in range(heads_per_step):
            h = hg * heads_per_step + g
            a_gate = gt_ref[nheads + h]
            b_gate = gt_ref[h]
            neg_rate = -jnp.exp(jnp.zeros_like(a_gate) + alog_ref[h])
            g_log = neg_rate * jax.nn.softplus(a_gate + dtb_ref[h])
            gcs[g] = _dot_hi(g_log, upper_f)
            bts[g] = _dot_hi(jax.nn.sigmoid(b_gate), eye_f)

    def to_col(row_vec):
        return jnp.sum(jnp.where(eye_first, jnp.broadcast_to(row_vec, (c, 2 * c)), 0.0), axis=1, keepdims=True)

    def prepare_step(i, dst, between):
        aus, qws, kws, kus, gcs = dst
        chains = [(i * chunks_per_iter + j, g) for j in range(chunks_per_iter) for g in range(heads_per_step)]
        rows = [pl.ds(pl.multiple_of(n * c, c), c) for n, _ in chains]
        cols = [slice(g * dh, (g + 1) * dh) for _, g in chains]
        every = range(len(chains))
        gc_row = [gcs[g, pl.ds(n, 1), :] for n, g in chains]
        gc_col = [to_col(r) for r in gc_row]
        beta_col = [to_col(bts[g, pl.ds(n, 1), :]) for n, g in chains]
        decay = [jnp.where(causal, jnp.exp(jnp.where(causal, gc_col[x] - gc_row[x], 0.0)), 0.0) for x in every]
        e_col = [jnp.exp(cv) for cv in gc_col]
        qc = [qs[rows[x], cols[x]] for x in every]
        kc = [ks[rows[x], cols[x]] for x in every]
        vc = [vs[rows[x], cols[x]] for x in every]
        kb = [kc[x] * beta_col[x] for x in every]

        kk = []
        k2_hi = []
        for x in every:
            kb_hi, kb_lo = _split_bf16(kb[x])
            hi, lo = _split_bf16(jnp.concatenate([kc[x], kc[x]], axis=0))
            k2_hi.append(hi)
            kk.append(lax.dot_general(jnp.concatenate([kb_hi, kb_lo.astype(BF16), kb_hi], axis=1),
                                      jnp.concatenate([hi, hi, lo.astype(BF16)], axis=1),
                                      NT_DIMS, preferred_element_type=F32))
        between()
        a_low = [jnp.where(strict, kk[x] * decay[x], 0.0) for x in every]
        inv = [eye_f - a for a in a_low]
        pw = [_dot3(_lhs3(a, first), _rhs3(a)) for a in a_low]
        between()
        span = 2
        while 2 * span < c:
            both = [_dot3(_lhs3(jnp.concatenate([inv[x], pw[x]], axis=0), first2), _rhs3(pw[x])) for x in every]
            inv = [inv[x] + both[x][:c] for x in every]
            pw = [b[c:] for b in both]
            span *= 2
            between()
        inv = [inv[x] + _dot3(_lhs3(inv[x], first), _rhs3(pw[x])) for x in every]
        between()
        sol = [_dot3(_lhs3(inv[x], first),
                     _rhs3(jnp.concatenate([vc[x] * beta_col[x], kb[x] * e_col[x]], axis=1))) for x in every]
        between()
        attn = [(lax.dot_general(qc[x].astype(BF16), k2_hi[x], NT_DIMS, preferred_element_type=F32)
                 * decay[x])[:, :c].astype(BF16) for x in every]
        u_b = [s[:, :dh].astype(BF16) for s in sol]
        w_b = [s[:, dh:].astype(BF16) for s in sol]
        kdt = [(kc[x] * jnp.exp(gc_row[x][:, c - 1:c] - gc_col[x])).T.astype(BF16) for x in every]
        attn_w = [jnp.dot(attn[x], w_b[x], preferred_element_type=F32) for x in every]
        attn_u = [jnp.dot(attn[x], u_b[x], preferred_element_type=F32) for x in every]
        kd_w = [jnp.dot(kdt[x], w_b[x], preferred_element_type=F32) for x in every]
        kd_u = [jnp.dot(kdt[x], u_b[x], preferred_element_type=F32) for x in every]
        for x, (n, g) in enumerate(chains):
            qws[rows[x], cols[x]] = (qc[x] * e_col[x] - attn_w[x]).astype(BF16)
            aus[rows[x], cols[x]] = attn_u[x]
            kws[g, n] = kd_w[x].astype(BF16)
            kus[g, n] = kd_u[x]

    def state_step(n, state, src):
        aus, qws, kws, kus, gcs = src
        rows = pl.ds(pl.multiple_of(n * c, c), c)
        heads = range(heads_per_step)
        cols = [slice(g * dh, (g + 1) * dh) for g in heads]
        state_b = [s.astype(BF16) for s in state]
        decay_last = [jnp.exp(gcs[g, pl.ds(n, 1), :][:, c - 1:c]) for g in heads]
        new_state = tuple(state[g] * decay_last[g] + kus[g, n]
                          - jnp.dot(kws[g, n], state_b[g], preferred_element_type=F32) for g in heads)
        for g in heads:
            o_c = aus[rows, cols[g]] + jnp.dot(qws[rows, cols[g]], state_b[g], preferred_element_type=F32)
            o_n = o_c * lax.rsqrt(jnp.mean(o_c * o_c, axis=-1, keepdims=True) + NORM_EPS)
            o_ref[rows, cols[g]] = (o_n * nw_ref[...] * _silu(z_ref[rows, cols[g]])).astype(o_ref.dtype)
        return new_state

    zero_state = tuple(jnp.zeros((dh, dh), F32) for _ in range(heads_per_step))
    n_iters = nchunks // chunks_per_iter

    def run(dst, src):
        if dst is None:
            lax.fori_loop(0, nchunks, lambda n, state: state_step(n, state, src), zero_state)
            return
        prologue()
        chunk_gates(dst[-1])
        if src is None:
            def only_prepare(i, carry):
                prepare_step(i, dst, lambda: None)
                return carry
            lax.fori_loop(0, n_iters, only_prepare, 0)
            return

        def both(i, state):
            progress = {"state": state, "done": 0}

            def between():
                if progress["done"] < chunks_per_iter:
                    progress["state"] = state_step(i * chunks_per_iter + progress["done"], progress["state"], src)
                    progress["done"] += 1

            prepare_step(i, dst, between)
            while progress["done"] < chunks_per_iter:
                between()
            return progress["state"]

        lax.fori_loop(0, n_iters, both, zero_state)

    inner = jnp.logical_and(step > 0, step < npairs)
    even = step % 2 == 0
    pl.when(step == 0)(functools.partial(run, set_a, None))
    pl.when(jnp.logical_and(inner, even))(functools.partial(run, set_a, set_b))
    pl.when(jnp.logical_and(inner, jnp.logical_not(even)))(functools.partial(run, set_b, set_a))
    pl.when(step == npairs)(functools.partial(run, None, set_a if (npairs - 1) % 2 == 0 else set_b))


def _gdn(proj, gates_t, seq, q_col, k_col, v_col, z_col, width, conv_w, a_log, dt_bias, norm_w,
         heads_per_step=2, chunks_per_iter=8):
    m = proj.shape[0]
    nb = m // seq
    gw = heads_per_step * GDN_HEAD_DIM
    nh = width // GDN_HEAD_DIM
    nchunks = seq // GDN_CHUNK
    chunks_per_iter = math.gcd(chunks_per_iter, nchunks)
    jq, jk, jv, jz = (col // gw for col in (q_col, k_col, v_col, z_col))
    wblocks = width // gw
    gt3 = gates_t.reshape(2 * nh, m // GDN_CHUNK, GDN_CHUNK)
    smem = pl.BlockSpec(memory_space=pltpu.SMEM)
    ngroups = nh // heads_per_step
    npairs = nb * ngroups

    def pair(s):
        sc = jnp.minimum(s, npairs - 1)
        return sc // ngroups, sc % ngroups

    def prev_pair(s):
        sp = jnp.maximum(s - 1, 0)
        return sp // ngroups, sp % ngroups

    def col_spec(j0, which=pair):
        return pl.BlockSpec((seq, gw), lambda s: (which(s)[0], j0 + which(s)[1]))

    def cw_spec(j0):
        return pl.BlockSpec((CONV_WIDTH, gw), lambda s: (0, j0 + pair(s)[1]))

    prepared = [pltpu.VMEM((seq, gw), F32), pltpu.VMEM((seq, gw), BF16),
                pltpu.VMEM((heads_per_step, nchunks, GDN_HEAD_DIM, GDN_HEAD_DIM), BF16),
                pltpu.VMEM((heads_per_step, nchunks, GDN_HEAD_DIM, GDN_HEAD_DIM), F32),
                pltpu.VMEM((heads_per_step, nchunks, 2 * GDN_CHUNK), F32)]
    return pl.pallas_call(
        functools.partial(_gdn_kernel, heads_per_step=heads_per_step, chunks_per_iter=chunks_per_iter,
                          ngroups=ngroups, npairs=npairs),
        grid=(npairs + 1,),
        in_specs=[smem, smem, col_spec(jq), col_spec(jk), col_spec(jv), col_spec(jz, prev_pair),
                  cw_spec(0), cw_spec(wblocks), cw_spec(2 * wblocks),
                  pl.BlockSpec((2 * nh, nchunks, GDN_CHUNK), lambda s: (0, pair(s)[0], 0)),
                  pl.BlockSpec((1, GDN_HEAD_DIM), lambda s: (0, 0))],
        out_specs=pl.BlockSpec((seq, gw), lambda s: prev_pair(s)),
        out_shape=jax.ShapeDtypeStruct((m, width), BF16),
        scratch_shapes=[pltpu.VMEM((seq, gw), F32), pltpu.VMEM((seq, gw), F32), pltpu.VMEM((seq, gw), F32),
                        pltpu.VMEM((heads_per_step, nchunks, 2 * GDN_CHUNK), F32)] + prepared + prepared,
        compiler_params=_cparams("arbitrary"),
        name="gdn",
    )(a_log, dt_bias, proj, proj, proj, proj, conv_w, conv_w, conv_w, gt3, norm_w.reshape(1, GDN_HEAD_DIM))


def _top_k_rows(s, k, payload=None):
    r = s.shape[0]
    rowi = lax.broadcasted_iota(I32, s.shape, 0)
    vals, picks = [], []
    for _ in range(k):
        mx = jnp.max(s, axis=0, keepdims=True)
        arg = jnp.min(jnp.where(s == mx, rowi, r), axis=0, keepdims=True)
        hit = rowi == arg
        vals.append(mx)
        if payload is None:
            picks.append(arg)
        else:
            picks.append(jnp.max(jnp.where(hit, payload, -1), axis=0, keepdims=True))
        s = jnp.where(hit, -jnp.inf, s)
    return jnp.concatenate(vals, axis=0), jnp.concatenate(picks, axis=0)


def _select_experts(s1_t, s2_t):
    kk = PEER_TOPK
    s1, i1 = _top_k_rows(s1_t, kk)
    s2, i2 = _top_k_rows(s2_t, kk)
    cand_parts, id_parts = [], []
    a = 0
    while kk // (a + 1) > 1:
        nb = kk // (a + 1)
        nbp = -(-nb // SUBLANES) * SUBLANES
        part = s1[a:a + 1, :] + s2[:nbp, :]
        if nbp > nb:
            part = jnp.where(lax.broadcasted_iota(I32, part.shape, 0) < nb, part, -jnp.inf)
        cand_parts.append(part)
        id_parts.append(i1[a:a + 1, :] * PEER_NKEYS + i2[:nbp, :])
        a += 1
    cand_parts.append(s1[a:, :] + s2[0:1, :])
    id_parts.append(i1[a:, :] * PEER_NKEYS + i2[0:1, :])
    best, ids = _top_k_rows(jnp.concatenate(cand_parts, axis=0), kk,
                            payload=jnp.concatenate(id_parts, axis=0))
    e = jnp.exp(best - best[0:1, :])
    gates = e / jnp.sum(e, axis=0, keepdims=True)
    return jnp.right_shift(ids, PEER_NKEYS.bit_length() - 1), jnp.bitwise_and(ids, PEER_NKEYS - 1), gates


def _ple_route_kernel(x_ref, wg_ref, bg_ref, p_ref, wp_ref, q_ref, keys_ref, v_ref,
                      o_ref, i1_ref, i2_ref, gt_ref, vb_ref):
    vb_ref[...] = v_ref[...].astype(BF16)
    k = x_ref.shape[1]
    kc = k // PEER_HEADS
    rt = q_ref.shape[0]
    acc = None
    for h in range(PEER_HEADS):
        ks = slice(h * kc, (h + 1) * kc)
        part = jnp.dot(x_ref[:, ks], wg_ref[ks, :].astype(BF16), preferred_element_type=F32)
        acc = part if acc is None else acc + part
        rows = slice(h * PEER_TOPK, (h + 1) * PEER_TOPK)
        for lg in range(rt // LANES):
            tok = slice(lg * LANES, (lg + 1) * LANES)
            scores = []
            for p in range(2):
                c0 = (h * 2 + p) * PEER_HALF
                scores.append(lax.dot_general(keys_ref[h, p].astype(BF16),
                                              q_ref[tok, c0:c0 + PEER_HALF].astype(BF16),
                                              NT_DIMS, preferred_element_type=F32))
            i1, i2, gates = _select_experts(*scores)
            i1_ref[rows, tok] = i1
            i2_ref[rows, tok] = i2
            gt_ref[rows, tok] = gates
    gate = jax.nn.sigmoid(acc + bg_ref[...])
    o_ref[...] = gate * jnp.dot(p_ref[...].astype(BF16), wp_ref[...].astype(BF16), preferred_element_type=F32)


def _ple_route(x1b, wg, bg, p, wp, q, keys, v_tab, tm=1024, tn=512):
    m, k = x1b.shape
    n = wg.shape[1]
    kp = p.shape[1]
    qd = q.shape[1]
    tm, tn = _tile(m, tm), _tile(n, tn)
    nj = n // tn
    rt = m // ((m // tm) * nj)
    assert rt % LANES == 0 and k % PEER_HEADS == 0, (rt, k)
    nslots = PEER_HEADS * PEER_TOPK
    table = pl.BlockSpec((nslots, rt), lambda i, j: (0, i * nj + j))
    ne, d = v_tab.shape
    er = ne // ((m // tm) * nj)
    assert er * (m // tm) * nj == ne and er % (2 * SUBLANES) == 0, (ne, m // tm, nj)
    experts = pl.BlockSpec((er, d), lambda i, j: (i * nj + j, 0))
    return pl.pallas_call(
        _ple_route_kernel,
        grid=(m // tm, nj),
        in_specs=[pl.BlockSpec((tm, k), lambda i, j: (i, 0)),
                  pl.BlockSpec((k, tn), lambda i, j: (0, j)),
                  pl.BlockSpec((1, tn), lambda i, j: (0, j)),
                  pl.BlockSpec((tm, kp), lambda i, j: (i, 0)),
                  pl.BlockSpec((kp, tn), lambda i, j: (0, j)),
                  pl.BlockSpec((rt, qd), lambda i, j: (i * nj + j, 0)),
                  pl.BlockSpec(keys.shape, lambda i, j: (0, 0, 0, 0)),
                  experts],
        out_specs=[pl.BlockSpec((tm, tn), lambda i, j: (i, j)), table, table, table, experts],
        out_shape=[jax.ShapeDtypeStruct((m, n), F32), jax.ShapeDtypeStruct((nslots, m), I32),
                   jax.ShapeDtypeStruct((nslots, m), I32), jax.ShapeDtypeStruct((nslots, m), F32),
                   jax.ShapeDtypeStruct((ne, d), BF16)],
        compiler_params=_cparams("parallel", "arbitrary"),
        name="ple_route",
    )(x1b, wg, bg.reshape(1, n), p, wp, q, keys, v_tab)


def _gates_kernel(i1t_ref, i2t_ref, gtt_ref, u_ref, o_ref, ub_ref, i1s, i2s, gts):
    tm = i1t_ref.shape[1]
    kk = PEER_TOPK
    ub_ref[...] = u_ref[...].astype(BF16)
    i1s[...] = i1t_ref[...].T
    i2s[...] = i2t_ref[...].T
    gts[...] = gtt_ref[...].T

    nslots = PEER_HEADS * kk
    key_iota = lax.broadcasted_iota(I32, (PEER_NKEYS, nslots), 0)

    def group(gi):
        base = pl.multiple_of(gi * GATE_GROUP, GATE_GROUP)
        tiles = []
        for tk in range(GATE_GROUP):
            i1r = jnp.broadcast_to(i1s[pl.ds(base + tk, 1), :], (PEER_NKEYS, nslots))
            i2r = jnp.broadcast_to(i2s[pl.ds(base + tk, 1), :], (PEER_NKEYS, nslots))
            gr = jnp.broadcast_to(gts[pl.ds(base + tk, 1), :], (PEER_NKEYS, nslots))
            a_t = jnp.where(key_iota == i1r, gr, 0.0).astype(BF16)
            b_t = jnp.where(key_iota == i2r, 1.0, 0.0).astype(BF16)
            tiles.append(lax.dot_general(a_t, b_t, NT_DIMS, preferred_element_type=F32))
        o_ref[gi] = jnp.swapaxes(jnp.stack(tiles, axis=0), 0, 1).astype(o_ref.dtype)

    def groups(it, carry):
        for k in range(ROUTE_GROUPS_PER_ITER):
            group(it * ROUTE_GROUPS_PER_ITER + k)
        return carry

    lax.fori_loop(0, tm // (GATE_GROUP * ROUTE_GROUPS_PER_ITER), groups, 0)


def _gates(i1t, i2t, gtt, u_tab, tm=128):
    nslots, m = i1t.shape
    tm = _tile(m, tm)
    nsteps = m // tm
    ne, d = u_tab.shape
    er = ne // nsteps
    assert er * nsteps == ne and er % (2 * SUBLANES) == 0, (ne, nsteps)
    table = pl.BlockSpec((er, d), lambda i: (i, 0))
    slots = pl.BlockSpec((nslots, tm), lambda i: (0, i))
    return pl.pallas_call(
        _gates_kernel,
        grid=(nsteps,),
        in_specs=[slots, slots, slots, table],
        out_specs=[pl.BlockSpec((tm // GATE_GROUP, PEER_NKEYS, GATE_GROUP, PEER_NKEYS), lambda i: (i, 0, 0, 0)),
                   table],
        out_shape=[jax.ShapeDtypeStruct((m // GATE_GROUP, PEER_NKEYS, GATE_GROUP, PEER_NKEYS), BF16),
                   jax.ShapeDtypeStruct((ne, d), BF16)],
        scratch_shapes=[pltpu.VMEM((tm, nslots), I32), pltpu.VMEM((tm, nslots), I32),
                        pltpu.VMEM((tm, nslots), F32)],
        compiler_params=_cparams("parallel"),
        name="peer_gates",
    )(i1t, i2t, gtt, u_tab)


def _peer_kernel(x_ref, u_ref, v_ref, g_ref, o_ref, w_scr, *, i1_per_step, nblk):
    s = pl.program_id(0)
    last = pl.num_programs(0) - 1
    tm = x_ref.shape[0]
    opens_tile = (s - 1) % nblk == 0

    def hidden():
        hid = lax.dot_general(x_ref[...], u_ref[...], NT_DIMS, preferred_element_type=F32)
        parts = []
        for gi in range(i1_per_step):
            gate = g_ref[:, gi].reshape(tm, PEER_NKEYS).astype(F32)
            parts.append((gate * _gelu(hid[:, gi * PEER_NKEYS:(gi + 1) * PEER_NKEYS])).astype(BF16))
        return jnp.concatenate(parts, axis=1)

    @pl.when(s == 0)
    def _():
        w_scr[...] = hidden()

    def both(first_of_tile):
        w_prev = w_scr[...]
        w_scr[...] = hidden()
        contrib = jnp.dot(w_prev, v_ref[...].astype(BF16), preferred_element_type=F32)
        if first_of_tile:
            o_ref[...] = contrib
        else:
            o_ref[...] += contrib

    inner = jnp.logical_and(s > 0, s < last)
    pl.when(jnp.logical_and(inner, opens_tile))(functools.partial(both, True))
    pl.when(jnp.logical_and(inner, jnp.logical_not(opens_tile)))(functools.partial(both, False))

    @pl.when(s == last)
    def _():
        o_ref[...] += jnp.dot(w_scr[...], v_ref[...].astype(BF16), preferred_element_type=F32)


def _peer(x1b, ub, vb, gates, tm=512, i1_per_step=4):
    m, d = x1b.shape
    ne = ub.shape[0]
    tm = _tile(m, tm)
    te = i1_per_step * PEER_NKEYS
    nblk = ne // te
    assert nblk > 1, "the last step accumulates: a token tile needs more than one expert block"
    npairs = (m // tm) * nblk

    def pair(s):
        sc = jnp.minimum(s, npairs - 1)
        return sc // nblk, sc % nblk

    def prev_pair(s):
        sp = jnp.maximum(s - 1, 0)
        return sp // nblk, sp % nblk

    return pl.pallas_call(
        functools.partial(_peer_kernel, i1_per_step=i1_per_step, nblk=nblk),
        grid=(npairs + 1,),
        in_specs=[pl.BlockSpec((tm, d), lambda s: (pair(s)[0], 0)),
                  pl.BlockSpec((te, d), lambda s: (pair(s)[1], 0)),
                  pl.BlockSpec((te, d), lambda s: (prev_pair(s)[1], 0)),
                  pl.BlockSpec((tm // GATE_GROUP, i1_per_step, GATE_GROUP, PEER_NKEYS),
                               lambda s: (pair(s)[0], pair(s)[1], 0, 0))],
        out_specs=pl.BlockSpec((tm, d), lambda s: (prev_pair(s)[0], 0)),
        out_shape=jax.ShapeDtypeStruct((m, d), F32),
        scratch_shapes=[pltpu.VMEM((tm, te), BF16)],
        compiler_params=_cparams("arbitrary"),
        name="peer_experts",
    )(x1b, ub, vb, gates)


def _layer(h, hb, p, seq, w_in, rg_conv_w, rg_conv_b, rg_wa, rg_ba, rg_wx, rg_bx, rg_lambda, rg_out,
           gdn_conv_w, gdn_a_log, gdn_dt_bias, gdn_norm_w, gdn_out, w_o, ln1_g, ln1_b, peer_wq, peer_keys,
           peer_u, peer_v, ple_w, ple_gate_w, ple_gate_b, ln2_g, ln2_b, alpha, last, entry):
    d = h.shape[1]
    rg_w = rg_out.shape[0]
    gdn_w = gdn_out.shape[0]
    nh = gdn_w // GDN_HEAD_DIM
    n_main = 2 * rg_w + 4 * gdn_w
    w_in_t = w_in.T
    cols = {"rg_x": 0, "rg_y": rg_w, "q": 2 * rg_w, "k": 2 * rg_w + gdn_w, "v": 2 * rg_w + 2 * gdn_w,
            "z": 2 * rg_w + 3 * gdn_w}

    proj = _matmul_wt(hb, w_in_t, 0, n_main, tm=1024, tn=512, name="in_proj")
    proj_m = _matmul_wt(hb, w_in_t, n_main + 2 * nh, 2 * d, tm=1024, tn=512, name="in_proj_merge")
    gates_t = _matmul_nt(w_in_t, n_main, 2 * nh, hb, tm=512)

    h_a = _rglru(proj, seq, cols["rg_x"], cols["rg_y"], rg_w, rg_conv_w, rg_conv_b, rg_wa, rg_ba,
                 rg_wx, rg_bx, rg_lambda)
    o_b = _gdn(proj, gates_t, seq, cols["q"], cols["k"], cols["v"], cols["z"], gdn_w, gdn_conv_w,
               gdn_a_log, gdn_dt_bias, gdn_norm_w)
    merged = _merge(h_a, o_b, rg_out, gdn_out, proj_m, 0, d)
    mix = _matmul(merged, w_o, tm=1024, tn=512, name="out_proj")
    x1, x1b = _add_ln([h, mix], ln1_g, ln1_b, alpha, (F32, BF16), entry=entry)

    q = _matmul(x1b, peer_wq, tm=1024, tn=512, name="peer_query")
    ple, i1t, i2t, gtt, peer_vb = _ple_route(x1b, ple_gate_w, ple_gate_b, p, ple_w, q, peer_keys, peer_v)
    gates, peer_ub = _gates(i1t, i2t, gtt, peer_u)
    ffn = _peer(x1b, peer_ub, peer_vb, gates)
    out = _add_ln([x1, ffn, ple], ln2_g, ln2_b, alpha, (F32,) if last else (F32, BF16))
    return (out[0], None) if last else tuple(out)


def kernel(x, p, ln_emb_g, ln_emb_b, w_in, rg_conv_w, rg_conv_b, rg_wa, rg_ba, rg_wx, rg_bx, rg_lambda, rg_out, gdn_conv_w, gdn_a_log, gdn_dt_bias, gdn_norm_w, gdn_out, w_o, ln1_g, ln1_b, peer_wq, peer_keys, peer_u, peer_v, ple_w, ple_gate_w, ple_gate_b, ln2_g, ln2_b):
    bsz, seq, d = x.shape
    depth = w_in.shape[0]
    m = bsz * seq
    alpha = (2.0 * depth) ** 0.25
    h = x.reshape(m, d)
    entry = (ln_emb_g, ln_emb_b)
    (hb,) = _add_ln([h], ln_emb_g, ln_emb_b, 1.0, (BF16,))
    for i in range(depth):
        h, hb = _layer(h, hb, p[i].reshape(m, -1), seq, w_in[i], rg_conv_w[i], rg_conv_b[i], rg_wa[i], rg_ba[i],
                       rg_wx[i], rg_bx[i], rg_lambda[i], rg_out[i], gdn_conv_w[i], gdn_a_log[i], gdn_dt_bias[i],
                       gdn_norm_w[i], gdn_out[i], w_o[i], ln1_g[i], ln1_b[i], peer_wq[i], peer_keys[i],
                       peer_u[i], peer_v[i], ple_w[i], ple_gate_w[i], ple_gate_b[i], ln2_g[i], ln2_b[i], alpha,
                       i == depth - 1, entry)
        entry = None
    return h.reshape(bsz, seq, d)
```

```python
import functools
import math

import jax
import jax.numpy as jnp
from jax import lax
from jax.experimental import pallas as pl
from jax.experimental.pallas import tpu as pltpu

F32 = jnp.float32
BF16 = jnp.bfloat16
I32 = jnp.int32

LANES = 128
SUBLANES = 8
VMEM_LIMIT = 58 * 1024 * 1024

CONV_WIDTH = 4
RG_BLOCK = 128
RG_C = 8.0
GDN_HEAD_DIM = 128
GDN_CHUNK = 64
PEER_HEADS = 8
PEER_NKEYS = 128
PEER_HALF = 128
PEER_TOPK = 16
GATE_GROUP = 16
ROUTE_GROUPS_PER_ITER = 4
LN_EPS = 1e-5
NORM_EPS = 1e-6
HIGHEST = lax.Precision.HIGHEST
NT_DIMS = (((1,), (1,)), ((), ()))


def _cparams(*sem):
    return pltpu.CompilerParams(dimension_semantics=sem, vmem_limit_bytes=VMEM_LIMIT)


def _tile(n, target, *also_divides):
    return math.gcd(target, n, *also_divides)


def _gelu(x):
    return 0.5 * x * (1.0 + lax.erf(x * (2.0 ** -0.5)))


def _silu(x):
    return x * jax.nn.sigmoid(x)


def _layer_norm(x, g, b):
    mu = jnp.mean(x, axis=-1, keepdims=True)
    xc = x - mu
    var = jnp.mean(xc * xc, axis=-1, keepdims=True)
    return xc * lax.rsqrt(var + LN_EPS) * g + b


def _add_ln_kernel(*refs, n_in, alpha, entry, out_dtypes):
    ins = refs[:n_in]
    params = refs[n_in:len(refs) - len(out_dtypes)]
    outs = refs[len(refs) - len(out_dtypes):]
    x = ins[0][...]
    if entry:
        x = _layer_norm(x, params[0][...], params[1][...])
    if alpha != 1.0:
        x = alpha * x
    for r in ins[1:]:
        x = x + r[...]
    y = _layer_norm(x, params[-2][...], params[-1][...])
    for o_ref in outs:
        o_ref[...] = y.astype(o_ref.dtype)


def _add_ln(ins, g, b, alpha, out_dtypes, entry=None, tm=256):
    m, d = ins[0].shape
    tm = _tile(m, tm)
    row = pl.BlockSpec((tm, d), lambda i: (i, 0))
    vec = pl.BlockSpec((1, d), lambda i: (0, 0))
    params = [t.reshape(1, d) for t in (tuple(entry) if entry else ()) + (g, b)]
    return pl.pallas_call(
        functools.partial(_add_ln_kernel, n_in=len(ins), alpha=alpha, entry=bool(entry),
                          out_dtypes=tuple(out_dtypes)),
        grid=(m // tm,),
        in_specs=[row] * len(ins) + [vec] * len(params),
        out_specs=[row] * len(out_dtypes),
        out_shape=[jax.ShapeDtypeStruct((m, d), dt) for dt in out_dtypes],
        compiler_params=_cparams("parallel"),
        name="add_ln",
    )(*ins, *params)


def _mm_kernel(a_ref, b_ref, o_ref):
    o_ref[...] = jnp.dot(a_ref[...], b_ref[...].astype(BF16), preferred_element_type=F32).astype(o_ref.dtype)


def _matmul(a, b, tm, tn, n=None, out_dtype=F32, name="matmul"):
    m, k = a.shape
    n = b.shape[1] if n is None else n
    tm, tn = _tile(m, tm), _tile(n, tn)
    return pl.pallas_call(
        _mm_kernel,
        grid=(m // tm, n // tn),
        in_specs=[pl.BlockSpec((tm, k), lambda i, j: (i, 0)),
                  pl.BlockSpec((k, tn), lambda i, j: (0, j))],
        out_specs=pl.BlockSpec((tm, tn), lambda i, j: (i, j)),
        out_shape=jax.ShapeDtypeStruct((m, n), out_dtype),
        compiler_params=_cparams("parallel", "arbitrary"),
        name=name,
    )(a, b)


def _mm_wt_kernel(a_ref, wt_ref, o_ref):
    o_ref[...] = lax.dot_general(a_ref[...], wt_ref[...].astype(BF16), NT_DIMS, preferred_element_type=F32)


def _matmul_wt(a, wt, row0, n, tm, tn, name):
    m, k = a.shape
    tm, tn = _tile(m, tm), _tile(n, tn)
    return pl.pallas_call(
        _mm_wt_kernel,
        grid=(m // tm, n // tn),
        in_specs=[pl.BlockSpec((tm, k), lambda i, j: (i, 0)),
                  pl.BlockSpec((pl.Element(tn), pl.Element(k)),
                               lambda i, j: (pl.multiple_of(row0 + j * tn, SUBLANES), 0))],
        out_specs=pl.BlockSpec((tm, tn), lambda i, j: (i, j)),
        out_shape=jax.ShapeDtypeStruct((m, n), F32),
        compiler_params=_cparams("parallel", "arbitrary"),
        name=name,
    )(a, wt)


def _mm_nt_kernel(w_ref, a_ref, o_ref):
    o_ref[...] = lax.dot_general(w_ref[...].astype(BF16), a_ref[...], NT_DIMS, preferred_element_type=F32)


def _matmul_nt(wt, row0, r, a, tm):
    k = wt.shape[1]
    m = a.shape[0]
    tm = _tile(m, tm)
    return pl.pallas_call(
        _mm_nt_kernel,
        grid=(m // tm,),
        in_specs=[pl.BlockSpec((pl.Element(r), pl.Element(k)), lambda i: (row0, 0)),
                  pl.BlockSpec((tm, k), lambda i: (i, 0))],
        out_specs=pl.BlockSpec((r, tm), lambda i: (0, i)),
        out_shape=jax.ShapeDtypeStruct((r, m), F32),
        compiler_params=_cparams("parallel"),
        name="gate_proj",
    )(wt, a)


def _merge_kernel(ha_ref, ob_ref, wa_ref, wb_ref, ma_ref, mb_ref, o_ref):
    ya = jnp.dot(ha_ref[...], wa_ref[...].astype(BF16), preferred_element_type=F32)
    yb = jnp.dot(ob_ref[...], wb_ref[...].astype(BF16), preferred_element_type=F32)
    o_ref[...] = (jax.nn.sigmoid(ma_ref[...]) * ya + jax.nn.sigmoid(mb_ref[...]) * yb).astype(o_ref.dtype)


def _merge(ha, ob, wa, wb, proj, ma_col, mb_col, tm=1024, tn=512):
    m, k = ha.shape
    n = wa.shape[1]
    tm, tn = _tile(m, tm), _tile(n, tn, ma_col, mb_col)
    ja, jb = ma_col // tn, mb_col // tn
    return pl.pallas_call(
        _merge_kernel,
        grid=(m // tm, n // tn),
        in_specs=[pl.BlockSpec((tm, k), lambda i, j: (i, 0)),
                  pl.BlockSpec((tm, k), lambda i, j: (i, 0)),
                  pl.BlockSpec((k, tn), lambda i, j: (0, j)),
                  pl.BlockSpec((k, tn), lambda i, j: (0, j)),
                  pl.BlockSpec((tm, tn), lambda i, j: (i, ja + j)),
                  pl.BlockSpec((tm, tn), lambda i, j: (i, jb + j))],
        out_specs=pl.BlockSpec((tm, tn), lambda i, j: (i, j)),
        out_shape=jax.ShapeDtypeStruct((m, n), BF16),
        compiler_params=_cparams("parallel", "arbitrary"),
        name="merge",
    )(ha, ob, wa, wb, proj, proj)


def _causal_conv(x, w):
    def taps(v, history):
        y = v * w[CONV_WIDTH - 1:CONV_WIDTH, :]
        for j in range(CONV_WIDTH - 1):
            y = y + history(v, CONV_WIDTH - 1 - j) * w[j:j + 1, :]
        return y

    head = x[:SUBLANES, :]
    head_row = lax.broadcasted_iota(I32, head.shape, 0)
    y_head = taps(head, lambda v, s: jnp.where(head_row >= s, pltpu.roll(v, s, 0), 0.0))
    y = taps(x, lambda v, s: pltpu.roll(v, s, 0))
    return jnp.concatenate([y_head, y[SUBLANES:, :]], axis=0)


def _rglru_kernel(x_ref, y_ref, cw_ref, cb_ref, wa_ref, ba_ref, wx_ref, bx_ref, lam_ref, o_ref):
    t, c = x_ref.shape
    row = lax.broadcasted_iota(I32, (t, c), 0)
    xr = _causal_conv(x_ref[...], cw_ref[...]) + cb_ref[...]
    xb = xr.astype(BF16)
    rs, is_ = [], []
    for k in range(c // RG_BLOCK):
        blk = xb[:, k * RG_BLOCK:(k + 1) * RG_BLOCK]
        rs.append(jnp.dot(blk, wa_ref[k], preferred_element_type=F32))
        is_.append(jnp.dot(blk, wx_ref[k], preferred_element_type=F32))
    r = jax.nn.sigmoid(jnp.concatenate(rs, axis=1) + ba_ref[...])
    i = jax.nn.sigmoid(jnp.concatenate(is_, axis=1) + bx_ref[...])
    log_a = (-RG_C) * r * jax.nn.softplus(-lam_ref[...])
    a = jnp.exp(log_a)
    b = jnp.sqrt(-jnp.tanh(log_a) * (a * a + 1.0)) * (i * xr)
    d = 1
    while d < t:
        keep = row >= d
        a_s = jnp.where(keep, pltpu.roll(a, d, 0), 1.0)
        b_s = jnp.where(keep, pltpu.roll(b, d, 0), 0.0)
        b = a * b_s + b
        a = a * a_s
        d *= 2
    o_ref[...] = (b * _gelu(y_ref[...])).astype(o_ref.dtype)


def _rglru(proj, seq, x_col, y_col, width, cw, cb, wa, ba, wx, bx, lam, ct=256):
    m = proj.shape[0]
    nb = m // seq
    ct = _tile(width, ct, x_col, y_col)
    jx, jy = x_col // ct, y_col // ct
    kb = ct // RG_BLOCK
    vec = pl.BlockSpec((1, ct), lambda b, c: (0, c))
    gate_w = pl.BlockSpec((kb, RG_BLOCK, RG_BLOCK), lambda b, c: (c, 0, 0))
    return pl.pallas_call(
        _rglru_kernel,
        grid=(nb, width // ct),
        in_specs=[pl.BlockSpec((seq, ct), lambda b, c: (b, jx + c)),
                  pl.BlockSpec((seq, ct), lambda b, c: (b, jy + c)),
                  pl.BlockSpec((CONV_WIDTH, ct), lambda b, c: (0, c)),
                  vec, gate_w, vec, gate_w, vec, vec],
        out_specs=pl.BlockSpec((seq, ct), lambda b, c: (b, c)),
        out_shape=jax.ShapeDtypeStruct((m, width), BF16),
        compiler_params=_cparams("parallel", "parallel"),
        name="rglru",
    )(proj, proj, cw, cb.reshape(1, width), wa.astype(BF16), ba.reshape(1, width),
      wx.astype(BF16), bx.reshape(1, width), lam.reshape(1, width))


def _dot_hi(a, b):
    return jnp.dot(a, b, precision=HIGHEST, preferred_element_type=F32)


def _split_bf16(x):
    hi = x.astype(BF16)
    return hi, x - hi.astype(F32)


def _lhs3(x_dup, first):
    _, lo = _split_bf16(x_dup)
    return jnp.concatenate([jnp.where(first, x_dup, lo).astype(BF16),
                            jnp.where(first, x_dup, 0.0).astype(BF16)], axis=1)


def _rhs3(p):
    hi, lo = _split_bf16(p)
    return jnp.concatenate([hi, hi, lo.astype(BF16), jnp.zeros_like(hi)], axis=0)


def _dot3(lhs3, rhs3):
    return jnp.dot(lhs3, rhs3, preferred_element_type=F32)


def _gdn_kernel(alog_ref, dtb_ref, q_ref, k_ref, v_ref, z_ref, cwq_ref, cwk_ref, cwv_ref, gt_ref, nw_ref,
                o_ref, qs, ks, vs, bts, *sets, heads_per_step, chunks_per_iter, ngroups, npairs):
    t, gw = q_ref.shape
    c = GDN_CHUNK
    dh = GDN_HEAD_DIM
    nchunks = t // c
    step = pl.program_id(0)
    hg = jnp.minimum(step, npairs - 1) % ngroups
    nheads = gt_ref.shape[0] // 2
    set_a, set_b = sets[:len(sets) // 2], sets[len(sets) // 2:]

    def prologue():
        q = _silu(_causal_conv(q_ref[...], cwq_ref[...]))
        k = _silu(_causal_conv(k_ref[...], cwk_ref[...]))
        vs[...] = _silu(_causal_conv(v_ref[...], cwv_ref[...]))
        for g in range(heads_per_step):
            sl = slice(g * dh, (g + 1) * dh)
            qh, kh = q[:, sl], k[:, sl]
            qs[:, sl] = qh * lax.rsqrt(jnp.sum(qh * qh, axis=-1, keepdims=True) + NORM_EPS) * (dh ** -0.5)
            ks[:, sl] = kh * lax.rsqrt(jnp.sum(kh * kh, axis=-1, keepdims=True) + NORM_EPS)

    ri = lax.broadcasted_iota(I32, (c, 2 * c), 0)
    lane = lax.broadcasted_iota(I32, (c, 2 * c), 1)
    ci = jnp.bitwise_and(lane, c - 1)
    first = lane < c
    first2 = lax.broadcasted_iota(I32, (2 * c, 2 * c), 1) < c
    eye_first = ri == lane
    causal = ri >= ci
    strict = ri > ci
    eye_f = (ri == ci).astype(F32)
    upper_f = (ri <= ci).astype(F32)

    def chunk_gates(gcs):
        for g in range(heads_per_step):
            h = hg * heads_per_step + g
            a_gate = gt_ref[nheads + h]
            b_gate = gt_ref[h]
            neg_rate = -jnp.exp(jnp.zeros_like(a_gate) + alog_ref[h])
            g_log = neg_rate * jax.nn.softplus(a_gate + dtb_ref[h])
            gcs[g] = _dot_hi(g_log, upper_f)
            bts[g] = _dot_hi(jax.nn.sigmoid(b_gate), eye_f)

    def to_col(row_vec):
        return jnp.sum(jnp.where(eye_first, jnp.broadcast_to(row_vec, (c, 2 * c)), 0.0), axis=1, keepdims=True)

    def prepare_step(i, dst, between):
        aus, qws, kws, kus, gcs = dst
        chains = [(i * chunks_per_iter + j, g) for j in range(chunks_per_iter) for g in range(heads_per_step)]
        rows = [pl.ds(pl.multiple_of(n * c, c), c) for n, _ in chains]
        cols = [slice(g * dh, (g + 1) * dh) for _, g in chains]
        every = range(len(chains))
        gc_row = [gcs[g, pl.ds(n, 1), :] for n, g in chains]
        gc_col = [to_col(r) for r in gc_row]
        beta_col = [to_col(bts[g, pl.ds(n, 1), :]) for n, g in chains]
        decay = [jnp.where(causal, jnp.exp(jnp.where(causal, gc_col[x] - gc_row[x], 0.0)), 0.0) for x in every]
        e_col = [jnp.exp(cv) for cv in gc_col]
        qc = [qs[rows[x], cols[x]] for x in every]
        kc = [ks[rows[x], cols[x]] for x in every]
        vc = [vs[rows[x], cols[x]] for x in every]
        kb = [kc[x] * beta_col[x] for x in every]

        kk = []
        k2_hi = []
        for x in every:
            kb_hi, kb_lo = _split_bf16(kb[x])
            hi, lo = _split_bf16(jnp.concatenate([kc[x], kc[x]], axis=0))
            k2_hi.append(hi)
            kk.append(lax.dot_general(jnp.concatenate([kb_hi, kb_lo.astype(BF16), kb_hi], axis=1),
                                      jnp.concatenate([hi, hi, lo.astype(BF16)], axis=1),
                                      NT_DIMS, preferred_element_type=F32))
        between()
        a_low = [jnp.where(strict, kk[x] * decay[x], 0.0) for x in every]
        inv = [eye_f - a for a in a_low]
        pw = [_dot3(_lhs3(a, first), _rhs3(a)) for a in a_low]
        between()
        span = 2
        while 2 * span < c:
            both = [_dot3(_lhs3(jnp.concatenate([inv[x], pw[x]], axis=0), first2), _rhs3(pw[x])) for x in every]
            inv = [inv[x] + both[x][:c] for x in every]
            pw = [b[c:] for b in both]
            span *= 2
            between()
        inv = [inv[x] + _dot3(_lhs3(inv[x], first), _rhs3(pw[x])) for x in every]
        between()
        sol = [_dot3(_lhs3(inv[x], first),
                     _rhs3(jnp.concatenate([vc[x] * beta_col[x], kb[x] * e_col[x]], axis=1))) for x in every]
        between()
        attn = [(lax.dot_general(qc[x].astype(BF16), k2_hi[x], NT_DIMS, preferred_element_type=F32)
                 * decay[x])[:, :c].astype(BF16) for x in every]
        u_b = [s[:, :dh].astype(BF16) for s in sol]
        w_b = [s[:, dh:].astype(BF16) for s in sol]
        kdt = [(kc[x] * jnp.exp(gc_row[x][:, c - 1:c] - gc_col[x])).T.astype(BF16) for x in every]
        attn_w = [jnp.dot(attn[x], w_b[x], preferred_element_type=F32) for x in every]
        attn_u = [jnp.dot(attn[x], u_b[x], preferred_element_type=F32) for x in every]
        kd_w = [jnp.dot(kdt[x], w_b[x], preferred_element_type=F32) for x in every]
        kd_u = [jnp.dot(kdt[x], u_b[x], preferred_element_type=F32) for x in every]
        for x, (n, g) in enumerate(chains):
            qws[rows[x], cols[x]] = (qc[x] * e_col[x] - attn_w[x]).astype(BF16)
            aus[rows[x], cols[x]] = attn_u[x]
            kws[g, n] = kd_w[x].astype(BF16)
            kus[g, n] = kd_u[x]

    def state_step(n, state, src):
        aus, qws, kws, kus, gcs = src
        rows = pl.ds(pl.multiple_of(n * c, c), c)
        heads = range(heads_per_step)
        cols = [slice(g * dh, (g + 1) * dh) for g in heads]
        state_b = [s.astype(BF16) for s in state]
        decay_last = [jnp.exp(gcs[g, pl.ds(n, 1), :][:, c - 1:c]) for g in heads]
        new_state = tuple(state[g] * decay_last[g] + kus[g, n]
                          - jnp.dot(kws[g, n], state_b[g], preferred_element_type=F32) for g in heads)
        for g in heads:
            o_c = aus[rows, cols[g]] + jnp.dot(qws[rows, cols[g]], state_b[g], preferred_element_type=F32)
            o_n = o_c * lax.rsqrt(jnp.mean(o_c * o_c, axis=-1, keepdims=True) + NORM_EPS)
            o_ref[rows, cols[g]] = (o_n * nw_ref[...] * _silu(z_ref[rows, cols[g]])).astype(o_ref.dtype)
        return new_state

    zero_state = tuple(jnp.zeros((dh, dh), F32) for _ in range(heads_per_step))
    n_iters = nchunks // chunks_per_iter

    def run(dst, src):
        if dst is None:
            lax.fori_loop(0, nchunks, lambda n, state: state_step(n, state, src), zero_state)
            return
        prologue()
        chunk_gates(dst[-1])
        if src is None:
            def only_prepare(i, carry):
                prepare_step(i, dst, lambda: None)
                return carry
            lax.fori_loop(0, n_iters, only_prepare, 0)
            return

        def both(i, state):
            progress = {"state": state, "done": 0}

            def between():
                if progress["done"] < chunks_per_iter:
                    progress["state"] = state_step(i * chunks_per_iter + progress["done"], progress["state"], src)
                    progress["done"] += 1

            prepare_step(i, dst, between)
            while progress["done"] < chunks_per_iter:
                between()
            return progress["state"]

        lax.fori_loop(0, n_iters, both, zero_state)

    inner = jnp.logical_and(step > 0, step < npairs)
    even = step % 2 == 0
    pl.when(step == 0)(functools.partial(run, set_a, None))
    pl.when(jnp.logical_and(inner, even))(functools.partial(run, set_a, set_b))
    pl.when(jnp.logical_and(inner, jnp.logical_not(even)))(functools.partial(run, set_b, set_a))
    pl.when(step == npairs)(functools.partial(run, None, set_a if (npairs - 1) % 2 == 0 else set_b))


def _gdn(proj, gates_t, seq, q_col, k_col, v_col, z_col, width, conv_w, a_log, dt_bias, norm_w,
         heads_per_step=2, chunks_per_iter=8):
    m = proj.shape[0]
    nb = m // seq
    gw = heads_per_step * GDN_HEAD_DIM
    nh = width // GDN_HEAD_DIM
    nchunks = seq // GDN_CHUNK
    chunks_per_iter = math.gcd(chunks_per_iter, nchunks)
    jq, jk, jv, jz = (col // gw for col in (q_col, k_col, v_col, z_col))
    wblocks = width // gw
    gt3 = gates_t.reshape(2 * nh, m // GDN_CHUNK, GDN_CHUNK)
    smem = pl.BlockSpec(memory_space=pltpu.SMEM)
    ngroups = nh // heads_per_step
    npairs = nb * ngroups

    def pair(s):
        sc = jnp.minimum(s, npairs - 1)
        return sc // ngroups, sc % ngroups

    def prev_pair(s):
        sp = jnp.maximum(s - 1, 0)
        return sp // ngroups, sp % ngroups

    def col_spec(j0, which=pair):
        return pl.BlockSpec((seq, gw), lambda s: (which(s)[0], j0 + which(s)[1]))

    def cw_spec(j0):
        return pl.BlockSpec((CONV_WIDTH, gw), lambda s: (0, j0 + pair(s)[1]))

    prepared = [pltpu.VMEM((seq, gw), F32), pltpu.VMEM((seq, gw), BF16),
                pltpu.VMEM((heads_per_step, nchunks, GDN_HEAD_DIM, GDN_HEAD_DIM), BF16),
                pltpu.VMEM((heads_per_step, nchunks, GDN_HEAD_DIM, GDN_HEAD_DIM), F32),
                pltpu.VMEM((heads_per_step, nchunks, 2 * GDN_CHUNK), F32)]
    return pl.pallas_call(
        functools.partial(_gdn_kernel, heads_per_step=heads_per_step, chunks_per_iter=chunks_per_iter,
                          ngroups=ngroups, npairs=npairs),
        grid=(npairs + 1,),
        in_specs=[smem, smem, col_spec(jq), col_spec(jk), col_spec(jv), col_spec(jz, prev_pair),
                  cw_spec(0), cw_spec(wblocks), cw_spec(2 * wblocks),
                  pl.BlockSpec((2 * nh, nchunks, GDN_CHUNK), lambda s: (0, pair(s)[0], 0)),
                  pl.BlockSpec((1, GDN_HEAD_DIM), lambda s: (0, 0))],
        out_specs=pl.BlockSpec((seq, gw), lambda s: prev_pair(s)),
        out_shape=jax.ShapeDtypeStruct((m, width), BF16),
        scratch_shapes=[pltpu.VMEM((seq, gw), F32), pltpu.VMEM((seq, gw), F32), pltpu.VMEM((seq, gw), F32),
                        pltpu.VMEM((heads_per_step, nchunks, 2 * GDN_CHUNK), F32)] + prepared + prepared,
        compiler_params=_cparams("arbitrary"),
        name="gdn",
    )(a_log, dt_bias, proj, proj, proj, proj, conv_w, conv_w, conv_w, gt3, norm_w.reshape(1, GDN_HEAD_DIM))


def _top_k_rows(s, k, payload=None):
    r = s.shape[0]
    rowi = lax.broadcasted_iota(I32, s.shape, 0)
    vals, picks = [], []
    for _ in range(k):
        mx = jnp.max(s, axis=0, keepdims=True)
        arg = jnp.min(jnp.where(s == mx, rowi, r), axis=0, keepdims=True)
        hit = rowi == arg
        vals.append(mx)
        if payload is None:
            picks.append(arg)
        else:
            picks.append(jnp.max(jnp.where(hit, payload, -1), axis=0, keepdims=True))
        s = jnp.where(hit, -jnp.inf, s)
    return jnp.concatenate(vals, axis=0), jnp.concatenate(picks, axis=0)


def _select_experts(s1_t, s2_t):
    kk = PEER_TOPK
    s1, i1 = _top_k_rows(s1_t, kk)
    s2, i2 = _top_k_rows(s2_t, kk)
    cand_parts, id_parts = [], []
    a = 0
    while kk // (a + 1) > 1:
        nb = kk // (a + 1)
        nbp = -(-nb // SUBLANES) * SUBLANES
        part = s1[a:a + 1, :] + s2[:nbp, :]
        if nbp > nb:
            part = jnp.where(lax.broadcasted_iota(I32, part.shape, 0) < nb, part, -jnp.inf)
        cand_parts.append(part)
        id_parts.append(i1[a:a + 1, :] * PEER_NKEYS + i2[:nbp, :])
        a += 1
    cand_parts.append(s1[a:, :] + s2[0:1, :])
    id_parts.append(i1[a:, :] * PEER_NKEYS + i2[0:1, :])
    best, ids = _top_k_rows(jnp.concatenate(cand_parts, axis=0), kk,
                            payload=jnp.concatenate(id_parts, axis=0))
    e = jnp.exp(best - best[0:1, :])
    gates = e / jnp.sum(e, axis=0, keepdims=True)
    return jnp.right_shift(ids, PEER_NKEYS.bit_length() - 1), jnp.bitwise_and(ids, PEER_NKEYS - 1), gates


def _ple_route_kernel(x_ref, wg_ref, bg_ref, p_ref, wp_ref, q_ref, keys_ref, v_ref,
                      o_ref, i1_ref, i2_ref, gt_ref, vb_ref):
    vb_ref[...] = v_ref[...].astype(BF16)
    k = x_ref.shape[1]
    kc = k // PEER_HEADS
    rt = q_ref.shape[0]
    acc = None
    for h in range(PEER_HEADS):
        ks = slice(h * kc, (h + 1) * kc)
        part = jnp.dot(x_ref[:, ks], wg_ref[ks, :].astype(BF16), preferred_element_type=F32)
        acc = part if acc is None else acc + part
        rows = slice(h * PEER_TOPK, (h + 1) * PEER_TOPK)
        for lg in range(rt // LANES):
            tok = slice(lg * LANES, (lg + 1) * LANES)
            scores = []
            for p in range(2):
                c0 = (h * 2 + p) * PEER_HALF
                scores.append(lax.dot_general(keys_ref[h, p].astype(BF16),
                                              q_ref[tok, c0:c0 + PEER_HALF].astype(BF16),
                                              NT_DIMS, preferred_element_type=F32))
            i1, i2, gates = _select_experts(*scores)
            i1_ref[rows, tok] = i1
            i2_ref[rows, tok] = i2
            gt_ref[rows, tok] = gates
    gate = jax.nn.sigmoid(acc + bg_ref[...])
    o_ref[...] = gate * jnp.dot(p_ref[...].astype(BF16), wp_ref[...].astype(BF16), preferred_element_type=F32)


def _ple_route(x1b, wg, bg, p, wp, q, keys, v_tab, tm=1024, tn=512):
    m, k = x1b.shape
    n = wg.shape[1]
    kp = p.shape[1]
    qd = q.shape[1]
    tm, tn = _tile(m, tm), _tile(n, tn)
    nj = n // tn
    rt = m // ((m // tm) * nj)
    assert rt % LANES == 0 and k % PEER_HEADS == 0, (rt, k)
    nslots = PEER_HEADS * PEER_TOPK
    table = pl.BlockSpec((nslots, rt), lambda i, j: (0, i * nj + j))
    ne, d = v_tab.shape
    er = ne // ((m // tm) * nj)
    assert er * (m // tm) * nj == ne and er % (2 * SUBLANES) == 0, (ne, m // tm, nj)
    experts = pl.BlockSpec((er, d), lambda i, j: (i * nj + j, 0))
    return pl.pallas_call(
        _ple_route_kernel,
        grid=(m // tm, nj),
        in_specs=[pl.BlockSpec((tm, k), lambda i, j: (i, 0)),
                  pl.BlockSpec((k, tn), lambda i, j: (0, j)),
                  pl.BlockSpec((1, tn), lambda i, j: (0, j)),
                  pl.BlockSpec((tm, kp), lambda i, j: (i, 0)),
                  pl.BlockSpec((kp, tn), lambda i, j: (0, j)),
                  pl.BlockSpec((rt, qd), lambda i, j: (i * nj + j, 0)),
                  pl.BlockSpec(keys.shape, lambda i, j: (0, 0, 0, 0)),
                  experts],
        out_specs=[pl.BlockSpec((tm, tn), lambda i, j: (i, j)), table, table, table, experts],
        out_shape=[jax.ShapeDtypeStruct((m, n), F32), jax.ShapeDtypeStruct((nslots, m), I32),
                   jax.ShapeDtypeStruct((nslots, m), I32), jax.ShapeDtypeStruct((nslots, m), F32),
                   jax.ShapeDtypeStruct((ne, d), BF16)],
        compiler_params=_cparams("parallel", "arbitrary"),
        name="ple_route",
    )(x1b, wg, bg.reshape(1, n), p, wp, q, keys, v_tab)


def _gates_kernel(i1t_ref, i2t_ref, gtt_ref, u_ref, o_ref, ub_ref, i1s, i2s, gts):
    tm = i1t_ref.shape[1]
    kk = PEER_TOPK
    ub_ref[...] = u_ref[...].astype(BF16)
    i1s[...] = i1t_ref[...].T
    i2s[...] = i2t_ref[...].T
    gts[...] = gtt_ref[...].T

    nslots = PEER_HEADS * kk
    key_iota = lax.broadcasted_iota(I32, (PEER_NKEYS, nslots), 0)

    def group(gi):
        base = pl.multiple_of(gi * GATE_GROUP, GATE_GROUP)
        tiles = []
        for tk in range(GATE_GROUP):
            i1r = jnp.broadcast_to(i1s[pl.ds(base + tk, 1), :], (PEER_NKEYS, nslots))
            i2r = jnp.broadcast_to(i2s[pl.ds(base + tk, 1), :], (PEER_NKEYS, nslots))
            gr = jnp.broadcast_to(gts[pl.ds(base + tk, 1), :], (PEER_NKEYS, nslots))
            a_t = jnp.where(key_iota == i1r, gr, 0.0).astype(BF16)
            b_t = jnp.where(key_iota == i2r, 1.0, 0.0).astype(BF16)
            tiles.append(lax.dot_general(a_t, b_t, NT_DIMS, preferred_element_type=F32))
        o_ref[gi] = jnp.swapaxes(jnp.stack(tiles, axis=0), 0, 1).astype(o_ref.dtype)

    def groups(it, carry):
        for k in range(ROUTE_GROUPS_PER_ITER):
            group(it * ROUTE_GROUPS_PER_ITER + k)
        return carry

    lax.fori_loop(0, tm // (GATE_GROUP * ROUTE_GROUPS_PER_ITER), groups, 0)


def _gates(i1t, i2t, gtt, u_tab, tm=128):
    nslots, m = i1t.shape
    tm = _tile(m, tm)
    nsteps = m // tm
    ne, d = u_tab.shape
    er = ne // nsteps
    assert er * nsteps == ne and er % (2 * SUBLANES) == 0, (ne, nsteps)
    table = pl.BlockSpec((er, d), lambda i: (i, 0))
    slots = pl.BlockSpec((nslots, tm), lambda i: (0, i))
    return pl.pallas_call(
        _gates_kernel,
        grid=(nsteps,),
        in_specs=[slots, slots, slots, table],
        out_specs=[pl.BlockSpec((tm // GATE_GROUP, PEER_NKEYS, GATE_GROUP, PEER_NKEYS), lambda i: (i, 0, 0, 0)),
                   table],
        out_shape=[jax.ShapeDtypeStruct((m // GATE_GROUP, PEER_NKEYS, GATE_GROUP, PEER_NKEYS), BF16),
                   jax.ShapeDtypeStruct((ne, d), BF16)],
        scratch_shapes=[pltpu.VMEM((tm, nslots), I32), pltpu.VMEM((tm, nslots), I32),
                        pltpu.VMEM((tm, nslots), F32)],
        compiler_params=_cparams("parallel"),
        name="peer_gates",
    )(i1t, i2t, gtt, u_tab)


def _peer_kernel(x_ref, u_ref, v_ref, g_ref, o_ref, w_scr, *, i1_per_step, nblk):
    s = pl.program_id(0)
    last = pl.num_programs(0) - 1
    tm = x_ref.shape[0]
    opens_tile = (s - 1) % nblk == 0

    def hidden():
        hid = lax.dot_general(x_ref[...], u_ref[...], NT_DIMS, preferred_element_type=F32)
        parts = []
        for gi in range(i1_per_step):
            gate = g_ref[:, gi].reshape(tm, PEER_NKEYS).astype(F32)
            parts.append((gate * _gelu(hid[:, gi * PEER_NKEYS:(gi + 1) * PEER_NKEYS])).astype(BF16))
        return jnp.concatenate(parts, axis=1)

    @pl.when(s == 0)
    def _():
        w_scr[...] = hidden()

    def both(first_of_tile):
        w_prev = w_scr[...]
        w_scr[...] = hidden()
        contrib = jnp.dot(w_prev, v_ref[...].astype(BF16), preferred_element_type=F32)
        if first_of_tile:
            o_ref[...] = contrib
        else:
            o_ref[...] += contrib

    inner = jnp.logical_and(s > 0, s < last)
    pl.when(jnp.logical_and(inner, opens_tile))(functools.partial(both, True))
    pl.when(jnp.logical_and(inner, jnp.logical_not(opens_tile)))(functools.partial(both, False))

    @pl.when(s == last)
    def _():
        o_ref[...] += jnp.dot(w_scr[...], v_ref[...].astype(BF16), preferred_element_type=F32)


def _peer(x1b, ub, vb, gates, tm=512, i1_per_step=4):
    m, d = x1b.shape
    ne = ub.shape[0]
    tm = _tile(m, tm)
    te = i1_per_step * PEER_NKEYS
    nblk = ne // te
    assert nblk > 1, "the last step accumulates: a token tile needs more than one expert block"
    npairs = (m // tm) * nblk

    def pair(s):
        sc = jnp.minimum(s, npairs - 1)
        return sc // nblk, sc % nblk

    def prev_pair(s):
        sp = jnp.maximum(s - 1, 0)
        return sp // nblk, sp % nblk

    return pl.pallas_call(
        functools.partial(_peer_kernel, i1_per_step=i1_per_step, nblk=nblk),
        grid=(npairs + 1,),
        in_specs=[pl.BlockSpec((tm, d), lambda s: (pair(s)[0], 0)),
                  pl.BlockSpec((te, d), lambda s: (pair(s)[1], 0)),
                  pl.BlockSpec((te, d), lambda s: (prev_pair(s)[1], 0)),
                  pl.BlockSpec((tm // GATE_GROUP, i1_per_step, GATE_GROUP, PEER_NKEYS),
                               lambda s: (pair(s)[0], pair(s)[1], 0, 0))],
        out_specs=pl.BlockSpec((tm, d), lambda s: (prev_pair(s)[0], 0)),
        out_shape=jax.ShapeDtypeStruct((m, d), F32),
        scratch_shapes=[pltpu.VMEM((tm, te), BF16)],
        compiler_params=_cparams("arbitrary"),
        name="peer_experts",
    )(x1b, ub, vb, gates)


def _layer(h, hb, p, seq, w_in, rg_conv_w, rg_conv_b, rg_wa, rg_ba, rg_wx, rg_bx, rg_lambda, rg_out,
           gdn_conv_w, gdn_a_log, gdn_dt_bias, gdn_norm_w, gdn_out, w_o, ln1_g, ln1_b, peer_wq, peer_keys,
           peer_u, peer_v, ple_w, ple_gate_w, ple_gate_b, ln2_g, ln2_b, alpha, last, entry):
    d = h.shape[1]
    rg_w = rg_out.shape[0]
    gdn_w = gdn_out.shape[0]
    nh = gdn_w // GDN_HEAD_DIM
    n_main = 2 * rg_w + 4 * gdn_w
    w_in_t = w_in.T
    cols = {"rg_x": 0, "rg_y": rg_w, "q": 2 * rg_w, "k": 2 * rg_w + gdn_w, "v": 2 * rg_w + 2 * gdn_w,
            "z": 2 * rg_w + 3 * gdn_w}

    proj = _matmul_wt(hb, w_in_t, 0, n_main, tm=1024, tn=512, name="in_proj")
    proj_m = _matmul_wt(hb, w_in_t, n_main + 2 * nh, 2 * d, tm=1024, tn=512, name="in_proj_merge")
    gates_t = _matmul_nt(w_in_t, n_main, 2 * nh, hb, tm=512)

    h_a = _rglru(proj, seq, cols["rg_x"], cols["rg_y"], rg_w, rg_conv_w, rg_conv_b, rg_wa, rg_ba,
                 rg_wx, rg_bx, rg_lambda)
    o_b = _gdn(proj, gates_t, seq, cols["q"], cols["k"], cols["v"], cols["z"], gdn_w, gdn_conv_w,
               gdn_a_log, gdn_dt_bias, gdn_norm_w)
    merged = _merge(h_a, o_b, rg_out, gdn_out, proj_m, 0, d)
    mix = _matmul(merged, w_o, tm=1024, tn=512, name="out_proj")
    x1, x1b = _add_ln([h, mix], ln1_g, ln1_b, alpha, (F32, BF16), entry=entry)

    q = _matmul(x1b, peer_wq, tm=1024, tn=512, name="peer_query")
    ple, i1t, i2t, gtt, peer_vb = _ple_route(x1b, ple_gate_w, ple_gate_b, p, ple_w, q, peer_keys, peer_v)
    gates, peer_ub = _gates(i1t, i2t, gtt, peer_u)
    ffn = _peer(x1b, peer_ub, peer_vb, gates)
    out = _add_ln([x1, ffn, ple], ln2_g, ln2_b, alpha, (F32,) if last else (F32, BF16))
    return (out[0], None) if last else tuple(out)


def kernel(x, p, ln_emb_g, ln_emb_b, w_in, rg_conv_w, rg_conv_b, rg_wa, rg_ba, rg_wx, rg_bx, rg_lambda, rg_out, gdn_conv_w, gdn_a_log, gdn_dt_bias, gdn_norm_w, gdn_out, w_o, ln1_g, ln1_b, peer_wq, peer_keys, peer_u, peer_v, ple_w, ple_gate_w, ple_gate_b, ln2_g, ln2_b):
    bsz, seq, d = x.shape
    depth = w_in.shape[0]
    m = bsz * seq
    alpha = (2.0 * depth) ** 0.25
    h = x.reshape(m, d)
    entry = (ln_emb_g, ln_emb_b)
    (hb,) = _add_ln([h], ln_emb_g, ln_emb_b, 1.0, (BF16,))
    for i in range(depth):
        h, hb = _layer(h, hb, p[i].reshape(m, -1), seq, w_in[i], rg_conv_w[i], rg_conv_b[i], rg_wa[i], rg_ba[i],
                       rg_wx[i], rg_bx[i], rg_lambda[i], rg_out[i], gdn_conv_w[i], gdn_a_log[i], gdn_dt_bias[i],
                       gdn_norm_w[i], gdn_out[i], w_o[i], ln1_g[i], ln1_b[i], peer_wq[i], peer_keys[i],
                       peer_u[i], peer_v[i], ple_w[i], ple_gate_w[i], ple_gate_b[i], ln2_g[i], ln2_b[i], alpha,
                       i == depth - 1, entry)
        entry = None
    return h.reshape(bsz, seq, d)
```
